```python
import math
import jax, jax.numpy as jnp
from jax import lax
import numpy as np

D_MODEL = 1024
BATCH = 2
SEQ = 16384
DEPTH = 4

CHUNK = 64
N_MIXERS = 3
EPS = 1e-6
DA_HEADS = 8
DA_HEAD_DIM = 64
DA_VALUE_DIM = 2 * DA_HEAD_DIM
DA_WIDTH = DA_HEADS * 2 * DA_HEAD_DIM
ROPE_THETA = 500000.0
ROPE_DIM = DA_HEAD_DIM // 4
Q_BLOCK = 128
SSM_INNER = 2 * D_MODEL
SSM_HEAD_DIM = 64
SSM_HEADS = SSM_INNER // SSM_HEAD_DIM
SSM_GROUPS = 8
SSM_STATE = 128
SSM_CONV = 4
SSM_CHUNK = CHUNK
SSM_BC = SSM_GROUPS * SSM_STATE
SSM_CONV_DIM = SSM_INNER + 2 * SSM_BC
SSM_IN_DIM = SSM_INNER + SSM_CONV_DIM + SSM_HEADS
S5_GROUP = 16
S5_GROUPS = D_MODEL // S5_GROUP
S5_STATE = 64
FFN_DIM = int(math.ceil(8 * D_MODEL / 3 / 128)) * 128
FFN_CONV = 3
N_A = len(range(0, DEPTH, N_MIXERS))
N_B = len(range(1, DEPTH, N_MIXERS))
N_C = len(range(2, DEPTH, N_MIXERS))

kernel_name = "interleaved_diffattn_ssd_s5_convffn"


def rmsnorm(x, g):
    xf = x.astype(jnp.float32)
    y = xf * lax.rsqrt(jnp.mean(xf * xf, axis=-1, keepdims=True) + EPS)
    return (y * g.astype(jnp.float32)).astype(x.dtype)


def causal_dwconv(x, w):
    k, c = w.shape
    return lax.conv_general_dilated(
        x, w[:, None, :], window_strides=(1,), padding=[(k - 1, 0)],
        dimension_numbers=("NWC", "WIO", "NWC"), feature_group_count=c)


def apply_rope(t, cos, sin):
    half = ROPE_DIM // 2
    c = cos[None, :, None, None, :]
    s = sin[None, :, None, None, :]
    t1 = t[..., :half]
    t2 = t[..., half:ROPE_DIM]
    out = jnp.concatenate([t1 * c - t2 * s, t2 * c + t1 * s, t[..., ROPE_DIM:]], axis=-1)
    return out.astype(t.dtype)


def diff_attention(h, w_qkv, w_o, lq1, lk1, lq2, lk2, sub_gain, lambda_init):
    b, s, _ = h.shape
    q, k, v = jnp.split(h @ w_qkv, 3, axis=-1)
    q = q.reshape(b, s, DA_HEADS, 2, DA_HEAD_DIM)
    k = k.reshape(b, s, DA_HEADS, 2, DA_HEAD_DIM)
    v = v.reshape(b, s, DA_HEADS, DA_VALUE_DIM)
    pos = jnp.arange(s, dtype=jnp.float32)
    inv_freq = ROPE_THETA ** (-jnp.arange(0, ROPE_DIM, 2, dtype=jnp.float32) / ROPE_DIM)
    ang = pos[:, None] * inv_freq[None, :]
    cos, sin = jnp.cos(ang), jnp.sin(ang)
    q = apply_rope(q, cos, sin)
    k = apply_rope(k, cos, sin)
    f32 = jnp.float32
    lam = (jnp.exp(jnp.sum(lq1.astype(f32) * lk1.astype(f32)))
           - jnp.exp(jnp.sum(lq2.astype(f32) * lk2.astype(f32))) + lambda_init)
    scale = DA_HEAD_DIM ** -0.5
    nb = s // Q_BLOCK
    qb = q.reshape(b, nb, Q_BLOCK, DA_HEADS, 2, DA_HEAD_DIM).transpose(1, 0, 2, 3, 4, 5)
    key_chunk = jnp.arange(s) // CHUNK

    def block(args):
        q_blk, blk = args
        sc = jnp.einsum("bqhcd,bkhcd->bhcqk", q_blk, k,
                        preferred_element_type=jnp.float32) * scale
        q_chunk = (blk * Q_BLOCK + jnp.arange(Q_BLOCK)) // CHUNK
        mask = key_chunk[None, :] <= q_chunk[:, None]
        p = jax.nn.softmax(jnp.where(mask, sc, -jnp.inf), axis=-1)
        attn = p[:, :, 0] - lam * p[:, :, 1]
        return jnp.einsum("bhqk,bkhe->bqhe", attn.astype(v.dtype), v)

    o = lax.map(block, (qb, jnp.arange(nb)))
    o = o.transpose(1, 0, 2, 3, 4).reshape(b, s, DA_HEADS, DA_VALUE_DIM)
    o = rmsnorm(o, sub_gain) * (1.0 - lambda_init)
    return o.reshape(b, s, DA_WIDTH) @ w_o


def ssd_chunked_scan(xdt, adt, bm, cm):
    b, s, nh, p = xdt.shape
    g = SSM_GROUPS
    r = nh // g
    L = SSM_CHUNK
    nc = s // L

    def to_chunks(t):
        return jnp.moveaxis(t.reshape((b, nc, L) + t.shape[2:]), 1, 0)

    xc = to_chunks(xdt.reshape(b, s, g, r, p))
    ac = to_chunks(adt.reshape(b, s, g, r))
    bc = to_chunks(bm)
    cc = to_chunks(cm)
    tril = jnp.tril(jnp.ones((L, L), dtype=bool))

    def step(state, inp):
        x_, a_, b_, c_ = inp
        acum = jnp.cumsum(a_, axis=1)
        seg = acum[:, :, None] - acum[:, None, :]
        decay_in = jnp.exp(jnp.where(tril[None, :, :, None, None], seg, -jnp.inf))
        cb = jnp.einsum("bign,bjgn->bijg", c_, b_)
        y_diag = jnp.einsum("bijg,bijgr,bjgrp->bigrp", cb, decay_in, x_)
        y_off = jnp.einsum("bign,bgrpn->bigrp", c_, state) * jnp.exp(acum)[..., None]
        last = acum[:, -1]
        decay_out = jnp.exp(last[:, None] - acum)
        new_state = (state * jnp.exp(last)[..., None, None]
                     + jnp.einsum("bjgn,bjgr,bjgrp->bgrpn", b_, decay_out, x_))
        return new_state, y_diag + y_off

    state0 = jnp.zeros((b, g, r, p, SSM_STATE), jnp.float32)
    _, ys = lax.scan(step, state0, (xc, ac, bc, cc))
    return jnp.moveaxis(ys, 0, 1).reshape(b, s, nh, p)


def mamba2_mixer(h, w_in, conv_w, conv_b, dt_bias, a_log, d_skip, norm_gain, w_out):
    b, s, _ = h.shape
    f32 = jnp.float32
    zxbcdt = h @ w_in
    z = zxbcdt[..., :SSM_INNER]
    xbc = zxbcdt[..., SSM_INNER:SSM_INNER + SSM_CONV_DIM]
    dt = zxbcdt[..., SSM_INNER + SSM_CONV_DIM:]
    xbc = jax.nn.silu(causal_dwconv(xbc, conv_w) + conv_b)
    xs = xbc[..., :SSM_INNER].reshape(b, s, SSM_HEADS, SSM_HEAD_DIM).astype(f32)
    bm = xbc[..., SSM_INNER:SSM_INNER + SSM_BC].reshape(b, s, SSM_GROUPS, SSM_STATE).astype(f32)
    cm = xbc[..., SSM_INNER + SSM_BC:].reshape(b, s, SSM_GROUPS, SSM_STATE).astype(f32)
    dt = jax.nn.softplus(dt.astype(f32) + dt_bias.astype(f32))
    a = -jnp.exp(a_log.astype(f32))
    y = ssd_chunked_scan(xs * dt[..., None], dt * a, bm, cm)
    y = y + d_skip.astype(f32)[:, None] * xs
    y = y.reshape(b, s, SSM_INNER) * jax.nn.silu(z.astype(f32))
    yg = y.reshape(b, s, SSM_GROUPS, SSM_INNER // SSM_GROUPS)
    yg = yg * lax.rsqrt(jnp.mean(yg * yg, axis=-1, keepdims=True) + EPS)
    y = yg.reshape(b, s, SSM_INNER) * norm_gain.astype(f32)
    return y.astype(h.dtype) @ w_out


def s5_mixer(h, a_re, a_im, log_dt, b_re, b_im, c_re, c_im, d_skip, w_glu):
    b, s, _ = h.shape
    f32 = jnp.float32
    u = h.astype(f32).reshape(b, s, S5_GROUPS, S5_GROUP)
    lam = lax.complex(jnp.minimum(a_re.astype(f32), -1e-4), a_im.astype(f32))
    delta = jnp.exp(log_dt.astype(f32))[:, None]
    lam_bar = jnp.exp(lam * delta)
    b_bar = ((lam_bar - 1.0) / lam)[..., None] * lax.complex(b_re.astype(f32), b_im.astype(f32))
    bu = jnp.einsum("gpk,bsgk->bsgp", b_bar, u.astype(jnp.complex64))

    def combine(e1, e2):
        a1, x1 = e1
        a2, x2 = e2
        return a2 * a1, a2 * x1 + x2

    _, states = lax.associative_scan(combine, (jnp.broadcast_to(lam_bar, bu.shape), bu), axis=1)
    c = lax.complex(c_re.astype(f32), c_im.astype(f32))
    y = jnp.einsum("gkp,bsgp->bsgk", c, states).real + d_skip.astype(f32) * u
    y = jax.nn.gelu(y.reshape(b, s, D_MODEL)).astype(h.dtype)
    val, gate = jnp.split(y @ w_glu, 2, axis=-1)
    return val * jax.nn.sigmoid(gate)


def conv_ffn(h, w_up, conv_w, conv_b, w_down):
    hu = causal_dwconv(h @ w_up, conv_w) + conv_b
    gate, val = jnp.split(hu, 2, axis=-1)
    return (jax.nn.silu(gate) * val) @ w_down


def setup_inputs(seed: int = 0) -> dict:
    key = jax.random.key(seed)
    ks = list(jax.random.split(key, 40))
    f32 = jnp.float32

    def nrm(shape, scale):
        return scale * jax.random.normal(ks.pop(), shape, f32)

    def unif(shape, lo, hi):
        return jax.random.uniform(ks.pop(), shape, f32, lo, hi)

    D = D_MODEL
    x = nrm((BATCH, SEQ, D), 1.0)
    norm_mix = 1.0 + nrm((DEPTH, D), 0.05)
    norm_ffn = 1.0 + nrm((DEPTH, D), 0.05)
    norm_final = 1.0 + nrm((D,), 0.05)
    attn_w_qkv = nrm((N_A, D, 3 * DA_WIDTH), D ** -0.5)
    attn_w_o = nrm((N_A, DA_WIDTH, D), DA_WIDTH ** -0.5)
    attn_lambda_q1 = nrm((N_A, DA_HEAD_DIM), 0.1)
    attn_lambda_k1 = nrm((N_A, DA_HEAD_DIM), 0.1)
    attn_lambda_q2 = nrm((N_A, DA_HEAD_DIM), 0.1)
    attn_lambda_k2 = nrm((N_A, DA_HEAD_DIM), 0.1)
    attn_sub_gain = 1.0 + nrm((N_A, DA_VALUE_DIM), 0.05)
    ssm_w_in = nrm((N_B, D, SSM_IN_DIM), D ** -0.5)
    ssm_conv_w = nrm((N_B, SSM_CONV, SSM_CONV_DIM), SSM_CONV ** -0.5)
    ssm_conv_b = nrm((N_B, SSM_CONV_DIM), 0.02)
    dt = jnp.maximum(jnp.exp(unif((N_B, SSM_HEADS), math.log(1e-3), math.log(1e-1))), 1e-4)
    ssm_dt_bias = dt + jnp.log(-jnp.expm1(-dt))
    ssm_a_log = jnp.log(unif((N_B, SSM_HEADS), 1.0, 16.0))
    ssm_d = 1.0 + nrm((N_B, SSM_HEADS), 0.1)
    ssm_norm_gain = 1.0 + nrm((N_B, SSM_INNER), 0.05)
    ssm_w_out = nrm((N_B, SSM_INNER, D), SSM_INNER ** -0.5)
    s5_a_re = -0.5 * jnp.exp(nrm((N_C, S5_GROUPS, S5_STATE), 0.05))
    s5_a_im = (math.pi * jnp.arange(S5_STATE, dtype=f32))[None, None, :] + nrm((N_C, S5_GROUPS, S5_STATE), 0.01)
    s5_log_dt = unif((N_C, S5_GROUPS), math.log(1e-3), math.log(1e-1))
    s5_b_re = nrm((N_C, S5_GROUPS, S5_STATE, S5_GROUP), (2 * S5_GROUP) ** -0.5)
    s5_b_im = nrm((N_C, S5_GROUPS, S5_STATE, S5_GROUP), (2 * S5_GROUP) ** -0.5)
    s5_c_re = nrm((N_C, S5_GROUPS, S5_GROUP, S5_STATE), S5_STATE ** -0.5)
    s5_c_im = nrm((N_C, S5_GROUPS, S5_GROUP, S5_STATE), S5_STATE ** -0.5)
    s5_d = nrm((N_C, S5_GROUPS, S5_GROUP), 0.5)
    s5_w_glu = nrm((N_C, D, 2 * D), D ** -0.5)
    ffn_w_up = nrm((DEPTH, D, 2 * FFN_DIM), D ** -0.5)
    ffn_conv_w = nrm((DEPTH, FFN_CONV, 2 * FFN_DIM), FFN_CONV ** -0.5)
    ffn_conv_b = nrm((DEPTH, 2 * FFN_DIM), 0.02)
    ffn_w_down = nrm((DEPTH, FFN_DIM, D), FFN_DIM ** -0.5)
    return {
        "x": x, "norm_mix": norm_mix, "norm_ffn": norm_ffn, "norm_final": norm_final,
        "attn_w_qkv": attn_w_qkv, "attn_w_o": attn_w_o,
        "attn_lambda_q1": attn_lambda_q1, "attn_lambda_k1": attn_lambda_k1,
        "attn_lambda_q2": attn_lambda_q2, "attn_lambda_k2": attn_lambda_k2,
        "attn_sub_gain": attn_sub_gain,
        "ssm_w_in": ssm_w_in, "ssm_conv_w": ssm_conv_w, "ssm_conv_b": ssm_conv_b,
        "ssm_dt_bias": ssm_dt_bias, "ssm_a_log": ssm_a_log, "ssm_d": ssm_d,
        "ssm_norm_gain": ssm_norm_gain, "ssm_w_out": ssm_w_out,
        "s5_a_re": s5_a_re, "s5_a_im": s5_a_im, "s5_log_dt": s5_log_dt,
        "s5_b_re": s5_b_re, "s5_b_im": s5_b_im, "s5_c_re": s5_c_re, "s5_c_im": s5_c_im,
        "s5_d": s5_d, "s5_w_glu": s5_w_glu,
        "ffn_w_up": ffn_w_up, "ffn_conv_w": ffn_conv_w, "ffn_conv_b": ffn_conv_b,
        "ffn_w_down": ffn_w_down,
    }


def reference(x, norm_mix, norm_ffn, norm_final,
              attn_w_qkv, attn_w_o, attn_lambda_q1, attn_lambda_k1,
              attn_lambda_q2, attn_lambda_k2, attn_sub_gain,
              ssm_w_in, ssm_conv_w, ssm_conv_b, ssm_dt_bias, ssm_a_log, ssm_d,
              ssm_norm_gain, ssm_w_out,
              s5_a_re, s5_a_im, s5_log_dt, s5_b_re, s5_b_im, s5_c_re, s5_c_im,
              s5_d, s5_w_glu,
              ffn_w_up, ffn_conv_w, ffn_conv_b, ffn_w_down):
    for i in range(DEPTH):
        kind = i % N_MIXERS
        j = i // N_MIXERS
        hn = rmsnorm(x, norm_mix[i])
        if kind == 0:
            lambda_init = 0.8 - 0.6 * math.exp(-0.3 * i)
            y = diff_attention(hn, attn_w_qkv[j], attn_w_o[j], attn_lambda_q1[j],
                               attn_lambda_k1[j], attn_lambda_q2[j], attn_lambda_k2[j],
                               attn_sub_gain[j], lambda_init)
        elif kind == 1:
            y = mamba2_mixer(hn, ssm_w_in[j], ssm_conv_w[j], ssm_conv_b[j], ssm_dt_bias[j],
                             ssm_a_log[j], ssm_d[j], ssm_norm_gain[j], ssm_w_out[j])
        else:
            y = s5_mixer(hn, s5_a_re[j], s5_a_im[j], s5_log_dt[j], s5_b_re[j], s5_b_im[j],
                         s5_c_re[j], s5_c_im[j], s5_d[j], s5_w_glu[j])
        x = x + y
        x = x + conv_ffn(rmsnorm(x, norm_ffn[i]), ffn_w_up[i], ffn_conv_w[i],
                         ffn_conv_b[i], ffn_w_down[i])
    return rmsnorm(x, norm_final)
```

```python
import functools
import math

import jax
import jax.numpy as jnp
from jax import lax
from jax.experimental import pallas as pl
from jax.experimental.pallas import tpu as pltpu

F32 = jnp.float32
BF16 = jnp.bfloat16

EPS = 1e-6
N_MIXERS = 3
CHUNK = 64
DA_HEADS = 8
DA_HEAD_DIM = 64
DA_VALUE_DIM = 2 * DA_HEAD_DIM
ROPE_THETA = 500000.0
ROPE_DIM = DA_HEAD_DIM // 4
SSM_HEAD_DIM = 64
SSM_GROUPS = 8
SSM_STATE = 128
SSM_CONV = 4
SSD_CHUNK = 128
S5_GROUP = 16
S5_STATE = 64
S5_CHUNK = 32
FFN_CONV = 3
FFN_COLS = 256
LANES = 128
SUBLANES = 8
MIB = 1024 * 1024


def _cparams(sem, vmem_mib):
    return pltpu.CompilerParams(dimension_semantics=sem, vmem_limit_bytes=vmem_mib * MIB)


def _resident(shape, index_map):
    return pl.BlockSpec(shape, index_map, pipeline_mode=pl.Buffered(1))


def _rms_rows(x, g):
    ms = jnp.mean(x * x, axis=-1, keepdims=True)
    return x * lax.rsqrt(ms + EPS) * g


def _sigmoid(x):
    return 1.0 / (1.0 + jnp.exp(-x))


def _silu(x):
    return x * _sigmoid(x)


def _idiv(v, n):
    return lax.shift_right_logical(v, int(math.log2(n)))


def _split3(x):
    hi = x.astype(BF16)
    r = x - hi.astype(F32)
    mid = r.astype(BF16)
    lo = (r - mid.astype(F32)).astype(BF16)
    return hi, mid, lo


def _dot(a, b):
    return jnp.dot(a, b, preferred_element_type=F32)


def _dot3_rhs01(x, e):
    hi, mid, lo = _split3(x)
    return _dot(hi, e) + _dot(mid, e) + _dot(lo, e)


def _dot3_lhs01(e, x):
    hi, mid, lo = _split3(x)
    return _dot(e, hi) + _dot(e, mid) + _dot(e, lo)


def _norm_kernel(x_ref, g_ref, o_ref):
    o_ref[...] = _rms_rows(x_ref[...], g_ref[...]).astype(o_ref.dtype)


def _rmsnorm(x2, g, out_dtype):
    t, d = x2.shape
    tm = min(512, t)
    return pl.pallas_call(
        _norm_kernel,
        out_shape=jax.ShapeDtypeStruct((t, d), out_dtype),
        grid=(t // tm,),
        in_specs=[pl.BlockSpec((tm, d), lambda i: (i, 0)),
                  pl.BlockSpec((1, d), lambda i: (0, 0))],
        out_specs=pl.BlockSpec((tm, d), lambda i: (i, 0)),
        compiler_params=_cparams(("parallel",), 32),
        name="rmsnorm",
    )(x2, g.reshape(1, d))


def _norm_matmul_kernel(x_ref, g_ref, w_ref, o_ref, hn_sc):
    @pl.when(pl.program_id(1) == 0)
    def _():
        hn_sc[...] = _rms_rows(x_ref[...], g_ref[...]).astype(BF16)

    o_ref[...] = _dot(hn_sc[...], w_ref[...]).astype(o_ref.dtype)


def _norm_matmul(x2, g, w, out_dtype, tn):
    t, d = x2.shape
    n = w.shape[1]
    tm = min(512, t)
    return pl.pallas_call(
        _norm_matmul_kernel,
        out_shape=jax.ShapeDtypeStruct((t, n), out_dtype),
        grid=(t // tm, n // tn),
        in_specs=[pl.BlockSpec((tm, d), lambda i, j: (i, 0)),
                  pl.BlockSpec((1, d), lambda i, j: (0, 0)),
                  pl.BlockSpec((d, tn), lambda i, j: (0, j))],
        out_specs=pl.BlockSpec((tm, tn), lambda i, j: (i, j)),
        scratch_shapes=[pltpu.VMEM((tm, d), BF16)],
        compiler_params=_cparams(("parallel", "arbitrary"), 40),
        name="norm_matmul",
    )(x2, g.reshape(1, d), w)


def _matmul_residual_kernel(a_ref, w_ref, r_ref, o_ref):
    o_ref[...] = r_ref[...] + _dot(a_ref[...], w_ref[...])


def _matmul_residual(a, w, res):
    t, k = a.shape
    n = w.shape[1]
    tm = min(512, t)
    return pl.pallas_call(
        _matmul_residual_kernel,
        out_shape=jax.ShapeDtypeStruct((t, n), F32),
        grid=(t // tm,),
        in_specs=[pl.BlockSpec((tm, k), lambda i: (i, 0)),
                  _resident((k, n), lambda i: (0, 0)),
                  pl.BlockSpec((tm, n), lambda i: (i, 0))],
        out_specs=pl.BlockSpec((tm, n), lambda i: (i, 0)),
        compiler_params=_cparams(("parallel",), 40),
        name="matmul_residual",
    )(a, w, res)


def _ffn_kernel(x_ref, g_ref, wup_ref, cw_ref, cb_ref, wdn_ref, o_ref,
                hn_sc, hbuf_sc, halo_sc, *, tm, ffn, steps_per_batch):
    cols = FFN_COLS
    nchunk = ffn // cols

    @pl.when(pl.program_id(0) % steps_per_batch == 0)
    def _():
        halo_sc[...] = jnp.zeros_like(halo_sc)

    x = x_ref[...]
    hn_sc[...] = _rms_rows(x, g_ref[...]).astype(BF16)
    o_ref[...] = x
    for c in range(nchunk):
        lo = c * cols
        hn = hn_sc[...]
        hbuf_sc[0:SUBLANES, :] = halo_sc[c]
        hbuf_sc[SUBLANES:SUBLANES + tm, 0:cols] = _dot(hn, wup_ref[:, lo:lo + cols])
        hbuf_sc[SUBLANES:SUBLANES + tm, cols:2 * cols] = _dot(hn, wup_ref[:, ffn + lo:ffn + lo + cols])
        halo_sc[c] = hbuf_sc[tm:tm + SUBLANES, :]
        w3 = jnp.concatenate([cw_ref[:, lo:lo + cols], cw_ref[:, ffn + lo:ffn + lo + cols]], axis=1)
        b = jnp.concatenate([cb_ref[:, lo:lo + cols], cb_ref[:, ffn + lo:ffn + lo + cols]], axis=1)
        y = b
        for k in range(FFN_CONV):
            off = SUBLANES - (FFN_CONV - 1) + k
            y = y + w3[k:k + 1, :] * hbuf_sc[off:off + tm, :]
        act = (_silu(y[:, :cols]) * y[:, cols:]).astype(BF16)
        o_ref[...] += _dot(act, wdn_ref[lo:lo + cols, :])


def _conv_ffn(x2, g, w_up, conv_w, conv_b, w_down, seq):
    t, d = x2.shape
    ffn = w_down.shape[0]
    tm = min(512, seq)
    kern = functools.partial(_ffn_kernel, tm=tm, ffn=ffn, steps_per_batch=seq // tm)
    return pl.pallas_call(
        kern,
        out_shape=jax.ShapeDtypeStruct((t, d), F32),
        grid=(t // tm,),
        in_specs=[pl.BlockSpec((tm, d), lambda i: (i, 0)),
                  _resident((1, d), lambda i: (0, 0)),
                  _resident((d, 2 * ffn), lambda i: (0, 0)),
                  _resident((FFN_CONV, 2 * ffn), lambda i: (0, 0)),
                  _resident((1, 2 * ffn), lambda i: (0, 0)),
                  _resident((ffn, d), lambda i: (0, 0))],
        out_specs=pl.BlockSpec((tm, d), lambda i: (i, 0)),
        scratch_shapes=[pltpu.VMEM((tm, d), BF16),
                        pltpu.VMEM((tm + SUBLANES, 2 * FFN_COLS), F32),
                        pltpu.VMEM((ffn // FFN_COLS, SUBLANES, 2 * FFN_COLS), F32)],
        compiler_params=_cparams(("arbitrary",), 48),
        name="conv_ffn",
    )(x2, g.reshape(1, d), w_up.astype(BF16), conv_w, conv_b.reshape(1, -1), w_down.astype(BF16))


def _qkv_kernel(x_ref, g_ref, w_ref, cos_ref, sa_ref, sb_ref, o_ref, hn_sc, *, tn, n_rope_tiles):
    j = pl.program_id(1)

    @pl.when(j == 0)
    def _():
        hn_sc[...] = _rms_rows(x_ref[...], g_ref[...]).astype(BF16)

    y = _dot(hn_sc[...], w_ref[...])

    @pl.when(j < n_rope_tiles)
    def _():
        reps = tn // LANES
        cos = jnp.concatenate([cos_ref[...]] * reps, axis=1)
        sa = jnp.concatenate([sa_ref[...]] * reps, axis=1)
        sb = jnp.concatenate([sb_ref[...]] * reps, axis=1)
        half = ROPE_DIM // 2
        y_up = pltpu.roll(y, tn - half, 1)
        y_dn = pltpu.roll(y, half, 1)
        o_ref[...] = (y * cos + y_up * sa + y_dn * sb).astype(o_ref.dtype)

    @pl.when(j >= n_rope_tiles)
    def _():
        o_ref[...] = y.astype(o_ref.dtype)


def _rope_tables(seq):
    half = ROPE_DIM // 2
    pos = jnp.arange(seq, dtype=F32)
    inv_freq = ROPE_THETA ** (-jnp.arange(0, ROPE_DIM, 2, dtype=F32) / ROPE_DIM)
    ang = pos[:, None] * inv_freq[None, :]
    cos, sin = jnp.cos(ang), jnp.sin(ang)
    lane = jnp.arange(LANES) % DA_HEAD_DIM
    idx = lane % half
    cos_l = jnp.where(lane[None, :] < ROPE_DIM, cos[:, idx], 1.0)
    sa_l = jnp.where(lane[None, :] < half, -sin[:, idx], 0.0)
    sb_l = jnp.where((lane[None, :] >= half) & (lane[None, :] < ROPE_DIM), sin[:, idx], 0.0)
    return cos_l, sa_l, sb_l


def _qkv_proj(x2, g, w_qkv, seq):
    t, d = x2.shape
    width = w_qkv.shape[1] // 3
    scale = DA_HEAD_DIM ** -0.5
    w = jnp.concatenate([w_qkv[:, :width] * scale, w_qkv[:, width:]], axis=1).astype(BF16)
    n = w.shape[1]
    tm = min(512, seq)
    tn = 512
    cos_l, sa_l, sb_l = _rope_tables(seq)
    spb = seq // tm
    kern = functools.partial(_qkv_kernel, tn=tn, n_rope_tiles=2 * width // tn)
    tab = pl.BlockSpec((tm, LANES), lambda i, j: (i % spb, 0))
    return pl.pallas_call(
        kern,
        out_shape=jax.ShapeDtypeStruct((t, n), BF16),
        grid=(t // tm, n // tn),
        in_specs=[pl.BlockSpec((tm, d), lambda i, j: (i, 0)),
                  pl.BlockSpec((1, d), lambda i, j: (0, 0)),
                  pl.BlockSpec((d, tn), lambda i, j: (0, j)),
                  tab, tab, tab],
        out_specs=pl.BlockSpec((tm, tn), lambda i, j: (i, j)),
        scratch_shapes=[pltpu.VMEM((tm, d), BF16)],
        compiler_params=_cparams(("parallel", "arbitrary"), 40),
        name="qkv_rope",
    )(x2, g.reshape(1, d), w, cos_l, sa_l, sb_l)


def _attn_kernel(q_ref, k_ref, v_ref, lq1_ref, lk1_ref, lq2_ref, lk2_ref, gain_ref, o_ref,
                 vt_sc, acc_sc, m_sc, l_sc, *, tq, seq, lambda_init):
    qi = pl.program_id(2)
    nkv = seq // tq

    @pl.when(qi == 0)
    def _():
        def tr(t, carry):
            st = pl.multiple_of(t * tq, tq)
            vt_sc[t] = v_ref[pl.ds(st, tq), :].astype(F32).T.astype(BF16)
            return carry
        lax.fori_loop(0, nkv, tr, 0)

    qt = q_ref[...].astype(F32).T
    row = lax.broadcasted_iota(jnp.int32, qt.shape, 0)
    qts = (jnp.where(row < DA_HEAD_DIM, qt, 0.0).astype(BF16),
           jnp.where(row >= DA_HEAD_DIM, qt, 0.0).astype(BF16))

    m_sc[...] = jnp.full_like(m_sc, -jnp.inf)
    l_sc[...] = jnp.zeros_like(l_sc)
    acc_sc[...] = jnp.zeros_like(acc_sc)

    def step(t, masked):
        st = pl.multiple_of(t * tq, tq)
        kt = k_ref[pl.ds(st, tq), :]
        vt = vt_sc[t]
        if masked:
            kc = _idiv(lax.broadcasted_iota(jnp.int32, (tq, tq), 0), CHUNK)
            qc = _idiv(lax.broadcasted_iota(jnp.int32, (tq, tq), 1), CHUNK)
            keep = kc <= qc
        for c in range(2):
            s = _dot(kt, qts[c])
            if masked:
                s = jnp.where(keep, s, -jnp.inf)
            m_prev = m_sc[c]
            m_new = jnp.maximum(m_prev, jnp.max(s, axis=0, keepdims=True))
            alpha = jnp.exp(m_prev - m_new)
            p = jnp.exp(s - m_new)
            l_sc[c] = alpha * l_sc[c] + jnp.sum(p, axis=0, keepdims=True)
            acc_sc[c] = alpha * acc_sc[c] + _dot(vt, p.astype(BF16))
            m_sc[c] = m_new

    def body(t, carry):
        step(t, False)
        return carry

    lax.fori_loop(0, qi, body, 0)
    step(qi, True)

    lam = (jnp.exp(jnp.sum(lq1_ref[...] * lk1_ref[...], axis=-1, keepdims=True))
           - jnp.exp(jnp.sum(lq2_ref[...] * lk2_ref[...], axis=-1, keepdims=True)) + lambda_init)
    o = acc_sc[0] / l_sc[0] - lam * (acc_sc[1] / l_sc[1])
    ms = jnp.mean(o * o, axis=0, keepdims=True)
    on = (o * lax.rsqrt(ms + EPS)).T
    o_ref[...] = (on * gain_ref[...] * (1.0 - lambda_init)).astype(o_ref.dtype)


def _diff_attention_core(qkv, lq1, lk1, lq2, lk2, sub_gain, lambda_init, batch, seq):
    width = DA_HEADS * DA_VALUE_DIM
    qkv3 = qkv.reshape(batch, seq, 3 * width)
    tq = min(512, seq)
    kern = functools.partial(_attn_kernel, tq=tq, seq=seq, lambda_init=lambda_init)
    vec = pl.BlockSpec((1, DA_HEAD_DIM), lambda b, h, i: (0, 0))
    return pl.pallas_call(
        kern,
        out_shape=jax.ShapeDtypeStruct((batch, seq, width), BF16),
        grid=(batch, DA_HEADS, seq // tq),
        in_specs=[pl.BlockSpec((None, tq, DA_VALUE_DIM), lambda b, h, i: (b, i, h)),
                  pl.BlockSpec((None, seq, DA_VALUE_DIM), lambda b, h, i: (b, 0, DA_HEADS + h)),
                  pl.BlockSpec((None, seq, DA_VALUE_DIM), lambda b, h, i: (b, 0, 2 * DA_HEADS + h)),
                  vec, vec, vec, vec,
                  pl.BlockSpec((1, DA_VALUE_DIM), lambda b, h, i: (0, 0))],
        out_specs=pl.BlockSpec((None, tq, DA_VALUE_DIM), lambda b, h, i: (b, i, h)),
        scratch_shapes=[pltpu.VMEM((seq // tq, DA_VALUE_DIM, tq), BF16),
                        pltpu.VMEM((2, DA_VALUE_DIM, tq), F32),
                        pltpu.VMEM((2, 1, tq), F32),
                        pltpu.VMEM((2, 1, tq), F32)],
        compiler_params=_cparams(("parallel", "parallel", "arbitrary"), 48),
        name="diff_attention",
    )(qkv3, qkv3, qkv3, lq1.reshape(1, -1), lk1.reshape(1, -1), lq2.reshape(1, -1),
      lk2.reshape(1, -1), sub_gain.reshape(1, -1)).reshape(batch * seq, width)


def _ssd_kernel(xbc_ref, z_ref, dt_ref, cw_ref, cb_ref, dtb_ref, alog_ref, dsk_ref, ng_ref,
                tril_ref, e128_ref, e64_ref, y_ref, hbuf_sc, state_sc, *, steps_per_batch, inner):
    L = SSD_CHUNK
    G = SSM_GROUPS
    N = SSM_STATE
    gw = inner // G
    R = gw // SSM_HEAD_DIM

    @pl.when(pl.program_id(0) % steps_per_batch == 0)
    def _():
        hbuf_sc[0:SUBLANES, :] = jnp.zeros((SUBLANES, hbuf_sc.shape[1]), F32)
        state_sc[...] = jnp.zeros_like(state_sc)

    hbuf_sc[SUBLANES:SUBLANES + L, :] = xbc_ref[...]
    conv = cb_ref[...]
    for k in range(SSM_CONV):
        off = SUBLANES - (SSM_CONV - 1) + k
        conv = conv + cw_ref[k:k + 1, :] * hbuf_sc[off:off + L, :]
    hbuf_sc[0:SUBLANES, :] = hbuf_sc[L:L + SUBLANES, :]
    xbc = _silu(conv)

    x = dt_ref[...] + dtb_ref[...]
    dt = jnp.maximum(x, 0.0) + jnp.log(1.0 + jnp.exp(-jnp.abs(x)))
    adt = dt * (-jnp.exp(alog_ref[...]))

    acum = _dot3_lhs01(tril_ref[...], adt)
    acol128 = _dot3_rhs01(acum, e128_ref[...])
    acol64 = _dot3_rhs01(acum, e64_ref[...])
    dt64 = _dot3_rhs01(dt, e64_ref[...])

    ri = lax.broadcasted_iota(jnp.int32, acol128.shape, 0)
    ci = lax.broadcasted_iota(jnp.int32, acol128.shape, 1) & (L - 1)
    arow = jnp.sum(jnp.where(ri == ci, acol128, 0.0), axis=0, keepdims=True)
    decay_in = jnp.exp(jnp.where(ci <= ri, acol128 - arow, -jnp.inf))

    xs = xbc[:, :inner]
    xdt = xs * dt64
    last = acol64[L - 1:L, :]
    xdo = (xdt * jnp.exp(last - acol64)).astype(BF16)
    xdt = xdt.astype(BF16)
    e_a = jnp.exp(acol64)
    e_last = jnp.exp(last)

    rr = _idiv(lax.broadcasted_iota(jnp.int32, (R * L, gw), 0), L)
    cc = _idiv(lax.broadcasted_iota(jnp.int32, (R * L, gw), 1), SSM_HEAD_DIM)
    blockdiag = rr == cc

    for g in range(G):
        bg = xbc[:, inner + g * N: inner + (g + 1) * N]
        cg = xbc[:, inner + G * N + g * N: inner + G * N + (g + 1) * N].astype(BF16)
        bgt = bg.T.astype(BF16)
        cb = _dot(cg, bgt)
        cbt = jnp.concatenate([cb] * R, axis=1)
        mg = (cbt * decay_in[:, g * R * L:(g + 1) * R * L]).astype(BF16)
        xg = xdt[:, g * gw:(g + 1) * gw]
        xbd = jnp.where(blockdiag, jnp.concatenate([xg] * R, axis=0), jnp.zeros((), BF16))
        y = _dot(mg, xbd)
        st = state_sc[g]
        y = y + _dot(cg, st.astype(BF16)) * e_a[:, g * gw:(g + 1) * gw]
        state_sc[g] = st * e_last[:, g * gw:(g + 1) * gw] + _dot(bgt, xdo[:, g * gw:(g + 1) * gw])
        y = y + dsk_ref[:, g * gw:(g + 1) * gw] * xs[:, g * gw:(g + 1) * gw]
        y = y * _silu(z_ref[:, g * gw:(g + 1) * gw])
        ms = jnp.mean(y * y, axis=-1, keepdims=True)
        y_ref[:, g * gw:(g + 1) * gw] = (y * lax.rsqrt(ms + EPS) * ng_ref[:, g * gw:(g + 1) * gw]).astype(y_ref.dtype)


def _mamba2_mixer(x2, g, w_in, conv_w, conv_b, dt_bias, a_log, d_skip, norm_gain, w_out, seq):
    t, d = x2.shape
    nh = dt_bias.shape[0]
    inner = nh * SSM_HEAD_DIM
    conv_dim = conv_w.shape[1]
    L = SSD_CHUNK
    w_xz = jnp.concatenate([w_in[:, inner:inner + conv_dim], w_in[:, :inner]], axis=1).astype(BF16)
    w_dt = jnp.pad(w_in[:, inner + conv_dim:], ((0, 0), (0, LANES - nh))).astype(BF16)
    xz = _norm_matmul(x2, g, w_xz, F32, 1024)
    dtr = _norm_matmul(x2, g, w_dt, F32, LANES)

    pad1 = lambda v: jnp.pad(v, (0, LANES - nh)).reshape(1, LANES)
    rows = jnp.arange(L)
    tril = (rows[None, :] <= rows[:, None]).astype(BF16)
    head = jnp.arange(LANES)
    e128 = (head[:, None] == (jnp.arange(nh * L) // L)[None, :]).astype(BF16)
    e64 = (head[:, None] == (jnp.arange(inner) // SSM_HEAD_DIM)[None, :]).astype(BF16)
    dsk = jnp.repeat(d_skip, SSM_HEAD_DIM).reshape(1, inner)

    nblk_x = conv_dim // inner
    kern = functools.partial(_ssd_kernel, steps_per_batch=seq // L, inner=inner)
    const = lambda shape: _resident(shape, lambda i: (0, 0))
    y = pl.pallas_call(
        kern,
        out_shape=jax.ShapeDtypeStruct((t, inner), BF16),
        grid=(t // L,),
        in_specs=[pl.BlockSpec((L, conv_dim), lambda i: (i, 0)),
                  pl.BlockSpec((L, inner), lambda i: (i, nblk_x)),
                  pl.BlockSpec((L, LANES), lambda i: (i, 0)),
                  const((SSM_CONV, conv_dim)), const((1, conv_dim)),
                  const((1, LANES)), const((1, LANES)),
                  const((1, inner)), const((1, inner)),
                  const((L, L)), const((LANES, nh * L)), const((LANES, inner))],
        out_specs=pl.BlockSpec((L, inner), lambda i: (i, 0)),
        scratch_shapes=[pltpu.VMEM((L + SUBLANES, conv_dim), F32),
                        pltpu.VMEM((SSM_GROUPS, SSM_STATE, inner // SSM_GROUPS), F32)],
        compiler_params=_cparams(("arbitrary",), 48),
        name="ssd_scan",
    )(xz, xz, dtr, conv_w, conv_b.reshape(1, -1), pad1(dt_bias), pad1(a_log), dsk,
      norm_gain.reshape(1, -1), tril, e128, e64)
    return _matmul_residual(y, w_out.astype(BF16), x2)


def _s5_tables(a_re, a_im, log_dt, b_re, b_im, c_re, c_im, d_skip):
    L = S5_CHUNK
    G, P = a_re.shape
    K = b_re.shape[-1]
    lam = lax.complex(jnp.minimum(a_re, -1e-4), a_im)
    delta = jnp.exp(log_dt)[:, None]
    lam_bar = jnp.exp(lam * delta)
    b_bar = ((lam_bar - 1.0) / lam)[..., None] * lax.complex(b_re, b_im)
    c = lax.complex(c_re, c_im)
    tau = jnp.arange(L + 1, dtype=F32)
    pw = jnp.exp(tau[:, None, None] * (lam * delta)[None])

    kt = jnp.einsum("gkp,tgp,gpi->gtki", c, pw[:L], b_bar).real
    lag = jnp.arange(L)[None, :] - jnp.arange(L)[:, None]
    kg = jnp.where((lag >= 0)[None, :, :, None, None], kt[:, jnp.maximum(lag, 0)], 0.0)
    m = kg.transpose(0, 1, 4, 2, 3).reshape(G, L * K, L * K)

    bst = pw[:L][::-1].transpose(1, 0, 2)[:, :, None, :] * b_bar.transpose(0, 2, 1)[:, None, :, :]
    bst = bst.reshape(G, L * K, P)
    w = c.transpose(0, 2, 1)[:, :, None, :] * pw[1:].transpose(1, 2, 0)[:, :, :, None]
    w = w.reshape(G, P, L * K)

    par = (jnp.arange(G) % 2)[:, None, None, None] == jnp.arange(2)[None, None, :, None]
    put_cols = lambda v: jnp.where(par, v[:, :, None, :], 0.0).reshape(G, L * K, 2 * P)
    par_r = (jnp.arange(G) % 2)[:, None, None, None] == jnp.arange(2)[None, :, None, None]
    put_rows = lambda v: jnp.where(par_r, v[:, None, :, :], 0.0).reshape(G, 2 * P, L * K)
    lam_l = pw[L].reshape(G // 2, 1, 2 * P)
    dsk = jnp.tile(d_skip[:, None, :], (1, L, 1)).reshape(G, 1, L * K)
    return dict(
        m=m.astype(BF16),
        bst_re=put_cols(bst.real).astype(BF16), bst_im=put_cols(bst.imag).astype(BF16),
        cst_re=put_rows(w.real).astype(BF16), cst_im=put_rows(-w.imag).astype(BF16),
        lam_re=lam_l.real, lam_im=lam_l.imag, dsk=dsk)


def _s5_kernel(u_ref, m_ref, bre_ref, bim_ref, cre_ref, cim_ref, lre_ref, lim_ref, dsk_ref, o_ref,
               s_re_sc, s_im_sc, in_re_sc, in_im_sc, *, batch, rows_per_batch):
    s_re_sc[...] = _dot(u_ref[0], bre_ref[0]) + _dot(u_ref[1], bre_ref[1])
    s_im_sc[...] = _dot(u_ref[0], bim_ref[0]) + _dot(u_ref[1], bim_ref[1])
    lre = lre_ref[0]
    lim = lim_ref[0]

    for b in range(batch):
        base = b * rows_per_batch

        def body(c, carry):
            sre, sim = carry
            in_re_sc[pl.ds(base + c, 1), :] = sre
            in_im_sc[pl.ds(base + c, 1), :] = sim
            nre = lre * sre - lim * sim + s_re_sc[pl.ds(base + c, 1), :]
            nim = lre * sim + lim * sre + s_im_sc[pl.ds(base + c, 1), :]
            return nre, nim

        zero = jnp.zeros((1, lre.shape[-1]), F32)
        lax.fori_loop(0, rows_per_batch, body, (zero, zero))

    sre = in_re_sc[...].astype(BF16)
    sim = in_im_sc[...].astype(BF16)
    for gi in range(2):
        u = u_ref[gi]
        y = _dot(u, m_ref[gi]) + _dot(sre, cre_ref[gi]) + _dot(sim, cim_ref[gi])
        y = y + dsk_ref[gi] * u.astype(F32)
        inner = math.sqrt(2.0 / math.pi) * (y + 0.044715 * (y * y * y))
        o_ref[gi] = (0.5 * y * (1.0 + jnp.tanh(inner))).astype(o_ref.dtype)


def _glu_residual_kernel(a_ref, w_ref, r_ref, o_ref, *, n):
    a = a_ref[...]
    val = _dot(a, w_ref[:, :n])
    gate = _dot(a, w_ref[:, n:])
    o_ref[...] = r_ref[...] + val * _sigmoid(gate)


def _s5_mixer(x2, g, a_re, a_im, log_dt, b_re, b_im, c_re, c_im, d_skip, w_glu, batch, seq):
    t, d = x2.shape
    L = S5_CHUNK
    G, K = d_skip.shape
    tab = _s5_tables(a_re, a_im, log_dt, b_re, b_im, c_re, c_im, d_skip)
    u = _rmsnorm(x2, g, BF16)
    rows = t // L
    lk = L * K
    ug = u.reshape(rows, L, G, K).transpose(2, 0, 1, 3).reshape(G, rows, lk)
    kern = functools.partial(_s5_kernel, batch=batch, rows_per_batch=seq // L)
    pair = lambda shape: pl.BlockSpec(shape, lambda i: (i, 0, 0))
    p2 = 2 * S5_STATE
    ya = pl.pallas_call(
        kern,
        out_shape=jax.ShapeDtypeStruct((G, rows, lk), BF16),
        grid=(G // 2,),
        in_specs=[pair((2, rows, lk)), pair((2, lk, lk)),
                  pair((2, lk, p2)), pair((2, lk, p2)),
                  pair((2, p2, lk)), pair((2, p2, lk)),
                  pair((1, 1, p2)), pair((1, 1, p2)), pair((2, 1, lk))],
        out_specs=pair((2, rows, lk)),
        scratch_shapes=[pltpu.VMEM((rows, p2), F32)] * 4,
        compiler_params=_cparams(("parallel",), 40),
        name="s5_scan",
    )(ug, tab["m"], tab["bst_re"], tab["bst_im"], tab["cst_re"], tab["cst_im"],
      tab["lam_re"], tab["lam_im"], tab["dsk"])
    ya = ya.reshape(G, rows, L, K).transpose(1, 2, 0, 3).reshape(t, d)
    tm = min(512, t)
    return pl.pallas_call(
        functools.partial(_glu_residual_kernel, n=d),
        out_shape=jax.ShapeDtypeStruct((t, d), F32),
        grid=(t // tm,),
        in_specs=[pl.BlockSpec((tm, d), lambda i: (i, 0)),
                  _resident((d, 2 * d), lambda i: (0, 0)),
                  pl.BlockSpec((tm, d), lambda i: (i, 0))],
        out_specs=pl.BlockSpec((tm, d), lambda i: (i, 0)),
        compiler_params=_cparams(("parallel",), 40),
        name="glu_residual",
    )(ya, w_glu.astype(BF16), x2)


def kernel(x, norm_mix, norm_ffn, norm_final, attn_w_qkv, attn_w_o, attn_lambda_q1, attn_lambda_k1, attn_lambda_q2, attn_lambda_k2, attn_sub_gain, ssm_w_in, ssm_conv_w, ssm_conv_b, ssm_dt_bias, ssm_a_log, ssm_d, ssm_norm_gain, ssm_w_out, s5_a_re, s5_a_im, s5_log_dt, s5_b_re, s5_b_im, s5_c_re, s5_c_im, s5_d, s5_w_glu, ffn_w_up, ffn_conv_w, ffn_conv_b, ffn_w_down):
    batch, seq, d = x.shape
    depth = norm_mix.shape[0]
    x2 = x.reshape(batch * seq, d)
    for i in range(depth):
        kind = i % N_MIXERS
        j = i // N_MIXERS
        if kind == 0:
            lambda_init = 0.8 - 0.6 * math.exp(-0.3 * i)
            qkv = _qkv_proj(x2, norm_mix[i], attn_w_qkv[j], seq)
            o = _diff_attention_core(qkv, attn_lambda_q1[j], attn_lambda_k1[j], attn_lambda_q2[j],
                                     attn_lambda_k2[j], attn_sub_gain[j], lambda_init, batch, seq)
            x2 = _matmul_residual(o, attn_w_o[j].astype(BF16), x2)
        elif kind == 1:
            x2 = _mamba2_mixer(x2, norm_mix[i], ssm_w_in[j], ssm_conv_w[j], ssm_conv_b[j], ssm_dt_bias[j],
                               ssm_a_log[j], ssm_d[j], ssm_norm_gain[j], ssm_w_out[j], seq)
        else:
            x2 = _s5_mixer(x2, norm_mix[i], s5_a_re[j], s5_a_im[j], s5_log_dt[j], s5_b_re[j], s5_b_im[j],
                           s5_c_re[j], s5_c_im[j], s5_d[j], s5_w_glu[j], batch, seq)
        x2 = _conv_ffn(x2, norm_ffn[i], ffn_w_up[i], ffn_conv_w[i], ffn_conv_b[i], ffn_w_down[i], seq)
    return _rmsnorm(x2, norm_final, F32).reshape(batch, seq, d)
```

```python
import functools
import math

import jax
import jax.numpy as jnp
from jax import lax
from jax.experimental import pallas as pl
from jax.experimental.pallas import tpu as pltpu

F32 = jnp.float32
BF16 = jnp.bfloat16

EPS = 1e-6
N_MIXERS = 3
CHUNK = 64
DA_HEADS = 8
DA_HEAD_DIM = 64
DA_VALUE_DIM = 2 * DA_HEAD_DIM
ROPE_THETA = 500000.0
ROPE_DIM = DA_HEAD_DIM // 4
SSM_HEAD_DIM = 64
SSM_GROUPS = 8
SSM_STATE = 128
SSM_CONV = 4
SSD_CHUNK = 128
S5_GROUP = 16
S5_STATE = 64
S5_CHUNK = 32
FFN_CONV = 3
FFN_COLS = 256
LANES = 128
SUBLANES = 8
MIB = 1024 * 1024


def _cparams(sem, vmem_mib):
    return pltpu.CompilerParams(dimension_semantics=sem, vmem_limit_bytes=vmem_mib * MIB)


def _resident(shape, index_map):
    return pl.BlockSpec(shape, index_map, pipeline_mode=pl.Buffered(1))


def _rms_rows(x, g):
    ms = jnp.mean(x * x, axis=-1, keepdims=True)
    return x * lax.rsqrt(ms + EPS) * g


def _sigmoid(x):
    return 1.0 / (1.0 + jnp.exp(-x))


def _silu(x):
    return x * _sigmoid(x)


def _idiv(v, n):
    return lax.shift_right_logical(v, int(math.log2(n)))


def _split3(x):
    hi = x.astype(BF16)
    r = x - hi.astype(F32)
    mid = r.astype(BF16)
    lo = (r - mid.astype(F32)).astype(BF16)
    return hi, mid, lo


def _dot(a, b):
    return jnp.dot(a, b, preferred_element_type=F32)


def _dot3_rhs01(x, e):
    hi, mid, lo = _split3(x)
    return _dot(hi, e) + _dot(mid, e) + _dot(lo, e)


def _dot3_lhs01(e, x):
    hi, mid, lo = _split3(x)
    return _dot(e, hi) + _dot(e, mid) + _dot(e, lo)


def _norm_kernel(x_ref, g_ref, o_ref):
    o_ref[...] = _rms_rows(x_ref[...], g_ref[...]).astype(o_ref.dtype)


def _rmsnorm(x2, g, out_dtype):
    t, d = x2.shape
    tm = min(512, t)
    return pl.pallas_call(
        _norm_kernel,
        out_shape=jax.ShapeDtypeStruct((t, d), out_dtype),
        grid=(t // tm,),
        in_specs=[pl.BlockSpec((tm, d), lambda i: (i, 0)),
                  pl.BlockSpec((1, d), lambda i: (0, 0))],
        out_specs=pl.BlockSpec((tm, d), lambda i: (i, 0)),
        compiler_params=_cparams(("parallel",), 32),
        name="rmsnorm",
    )(x2, g.reshape(1, d))


def _norm_matmul_kernel(x_ref, g_ref, w_ref, o_ref, hn_sc):
    @pl.when(pl.program_id(1) == 0)
    def _():
        hn_sc[...] = _rms_rows(x_ref[...], g_ref[...]).astype(BF16)

    o_ref[...] = _dot(hn_sc[...], w_ref[...]).astype(o_ref.dtype)


def _norm_matmul(x2, g, w, out_dtype, tn):
    t, d = x2.shape
    n = w.shape[1]
    tm = min(512, t)
    return pl.pallas_call(
        _norm_matmul_kernel,
        out_shape=jax.ShapeDtypeStruct((t, n), out_dtype),
        grid=(t // tm, n // tn),
        in_specs=[pl.BlockSpec((tm, d), lambda i, j: (i, 0)),
                  pl.BlockSpec((1, d), lambda i, j: (0, 0)),
                  pl.BlockSpec((d, tn), lambda i, j: (0, j))],
        out_specs=pl.BlockSpec((tm, tn), lambda i, j: (i, j)),
        scratch_shapes=[pltpu.VMEM((tm, d), BF16)],
        compiler_params=_cparams(("parallel", "arbitrary"), 40),
        name="norm_matmul",
    )(x2, g.reshape(1, d), w)


def _matmul_residual_kernel(a_ref, w_ref, r_ref, o_ref):
    o_ref[...] = r_ref[...] + _dot(a_ref[...], w_ref[...])


def _matmul_residual(a, w, res):
    t, k = a.shape
    n = w.shape[1]
    tm = min(512, t)
    return pl.pallas_call(
        _matmul_residual_kernel,
        out_shape=jax.ShapeDtypeStruct((t, n), F32),
        grid=(t // tm,),
        in_specs=[pl.BlockSpec((tm, k), lambda i: (i, 0)),
                  _resident((k, n), lambda i: (0, 0)),
                  pl.BlockSpec((tm, n), lambda i: (i, 0))],
        out_specs=pl.BlockSpec((tm, n), lambda i: (i, 0)),
        compiler_params=_cparams(("parallel",), 40),
        name="matmul_residual",
    )(a, w, res)


def _ffn_kernel(x_ref, g_ref, wup_ref, cw_ref, cb_ref, wdn_ref, o_ref,
                hn_sc, hbuf_sc, halo_sc, *, tm, ffn, steps_per_batch):
    cols = FFN_COLS
    nchunk = ffn // cols

    @pl.when(pl.program_id(0) % steps_per_batch == 0)
    def _():
        halo_sc[...] = jnp.zeros_like(halo_sc)

    x = x_ref[...]
    hn_sc[...] = _rms_rows(x, g_ref[...]).astype(BF16)
    o_ref[...] = x
    for c in range(nchunk):
        lo = c * cols
        hn = hn_sc[...]
        hbuf_sc[0:SUBLANES, :] = halo_sc[c]
        hbuf_sc[SUBLANES:SUBLANES + tm, 0:cols] = _dot(hn, wup_ref[:, lo:lo + cols])
        hbuf_sc[SUBLANES:SUBLANES + tm, cols:2 * cols] = _dot(hn, wup_ref[:, ffn + lo:ffn + lo + cols])
        halo_sc[c] = hbuf_sc[tm:tm + SUBLANES, :]
        w3 = jnp.concatenate([cw_ref[:, lo:lo + cols], cw_ref[:, ffn + lo:ffn + lo + cols]], axis=1)
        b = jnp.concatenate([cb_ref[:, lo:lo + cols], cb_ref[:, ffn + lo:ffn + lo + cols]], axis=1)
        y = b
        for k in range(FFN_CONV):
            off = SUBLANES - (FFN_CONV - 1) + k
            y = y + w3[k:k + 1, :] * hbuf_sc[off:off + tm, :]
        act = (_silu(y[:, :cols]) * y[:, cols:]).astype(BF16)
        o_ref[...] += _dot(act, wdn_ref[lo:lo + cols, :])


def _conv_ffn(x2, g, w_up, conv_w, conv_b, w_down, seq):
    t, d = x2.shape
    ffn = w_down.shape[0]
    tm = min(512, seq)
    kern = functools.partial(_ffn_kernel, tm=tm, ffn=ffn, steps_per_batch=seq // tm)
    return pl.pallas_call(
        kern,
        out_shape=jax.ShapeDtypeStruct((t, d), F32),
        grid=(t // tm,),
        in_specs=[pl.BlockSpec((tm, d), lambda i: (i, 0)),
                  _resident((1, d), lambda i: (0, 0)),
                  _resident((d, 2 * ffn), lambda i: (0, 0)),
                  _resident((FFN_CONV, 2 * ffn), lambda i: (0, 0)),
                  _resident((1, 2 * ffn), lambda i: (0, 0)),
                  _resident((ffn, d), lambda i: (0, 0))],
        out_specs=pl.BlockSpec((tm, d), lambda i: (i, 0)),
        scratch_shapes=[pltpu.VMEM((tm, d), BF16),
                        pltpu.VMEM((tm + SUBLANES, 2 * FFN_COLS), F32),
                        pltpu.VMEM((ffn // FFN_COLS, SUBLANES, 2 * FFN_COLS), F32)],
        compiler_params=_cparams(("arbitrary",), 48),
        name="conv_ffn",
    )(x2, g.reshape(1, d), w_up.astype(BF16), conv_w, conv_b.reshape(1, -1), w_down.astype(BF16))


def _qkv_kernel(x_ref, g_ref, w_ref, cos_ref, sa_ref, sb_ref, o_ref, hn_sc, *, tn, n_rope_tiles):
    j = pl.program_id(1)

    @pl.when(j == 0)
    def _():
        hn_sc[...] = _rms_rows(x_ref[...], g_ref[...]).astype(BF16)

    y = _dot(hn_sc[...], w_ref[...])

    @pl.when(j < n_rope_tiles)
    def _():
        reps = tn // LANES
        cos = jnp.concatenate([cos_ref[...]] * reps, axis=1)
        sa = jnp.concatenate([sa_ref[...]] * reps, axis=1)
        sb = jnp.concatenate([sb_ref[...]] * reps, axis=1)
        half = ROPE_DIM // 2
        y_up = pltpu.roll(y, tn - half, 1)
        y_dn = pltpu.roll(y, half, 1)
        o_ref[...] = (y * cos + y_up * sa + y_dn * sb).astype(o_ref.dtype)

    @pl.when(j >= n_rope_tiles)
    def _():
        o_ref[...] = y.astype(o_ref.dtype)


def _rope_tables(seq):
    half = ROPE_DIM // 2
    pos = jnp.arange(seq, dtype=F32)
    inv_freq = ROPE_THETA ** (-jnp.arange(0, ROPE_DIM, 2, dtype=F32) / ROPE_DIM)
    ang = pos[:, None] * inv_freq[None, :]
    cos, sin = jnp.cos(ang), jnp.sin(ang)
    lane = jnp.arange(LANES) % DA_HEAD_DIM
    idx = lane % half
    cos_l = jnp.where(lane[None, :] < ROPE_DIM, cos[:, idx], 1.0)
    sa_l = jnp.where(lane[None, :] < half, -sin[:, idx], 0.0)
    sb_l = jnp.where((lane[None, :] >= half) & (lane[None, :] < ROPE_DIM), sin[:, idx], 0.0)
    return cos_l, sa_l, sb_l


def _qkv_proj(x2, g, w_qkv, seq):
    t, d = x2.shape
    width = w_qkv.shape[1] // 3
    scale = DA_HEAD_DIM ** -0.5 * math.log2(math.e)
    w = jnp.concatenate([w_qkv[:, :width] * scale, w_qkv[:, width:]], axis=1).astype(BF16)
    n = w.shape[1]
    tm = min(512, seq)
    tn = 512
    cos_l, sa_l, sb_l = _rope_tables(seq)
    spb = seq // tm
    kern = functools.partial(_qkv_kernel, tn=tn, n_rope_tiles=2 * width // tn)
    tab = pl.BlockSpec((tm, LANES), lambda i, j: (i % spb, 0))
    return pl.pallas_call(
        kern,
        out_shape=jax.ShapeDtypeStruct((t, n), BF16),
        grid=(t // tm, n // tn),
        in_specs=[pl.BlockSpec((tm, d), lambda i, j: (i, 0)),
                  pl.BlockSpec((1, d), lambda i, j: (0, 0)),
                  pl.BlockSpec((d, tn), lambda i, j: (0, j)),
                  tab, tab, tab],
        out_specs=pl.BlockSpec((tm, tn), lambda i, j: (i, j)),
        scratch_shapes=[pltpu.VMEM((tm, d), BF16)],
        compiler_params=_cparams(("parallel", "arbitrary"), 40),
        name="qkv_rope",
    )(x2, g.reshape(1, d), w, cos_l, sa_l, sb_l)


def _attn_kernel(q_ref, k_ref, v_ref, lq1_ref, lk1_ref, lq2_ref, lk2_ref, gain_ref, o_ref,
                 vt_sc, qt_sc, s_sc, mx_sc, acc_sc, m_sc, l_sc, *, tq, seq, lambda_init):
    qi = pl.program_id(2)
    nkv = seq // tq

    @pl.when(qi == 0)
    def _():
        def tr(t, carry):
            st = pl.multiple_of(t * tq, tq)
            vt_sc[t] = v_ref[pl.ds(st, tq), :].astype(F32).T.astype(BF16)
            return carry
        lax.fori_loop(0, nkv, tr, 0)

    qt = q_ref[...].astype(F32).T
    row = lax.broadcasted_iota(jnp.int32, qt.shape, 0)
    qt_sc[0] = jnp.where(row < DA_HEAD_DIM, qt, 0.0).astype(BF16)
    qt_sc[1] = jnp.where(row >= DA_HEAD_DIM, qt, 0.0).astype(BF16)

    m_sc[...] = jnp.full_like(m_sc, -jnp.inf)
    l_sc[...] = jnp.zeros_like(l_sc)
    acc_sc[...] = jnp.zeros_like(acc_sc)

    def scores(t, slot, masked):
        st = pl.multiple_of(t * tq, tq)
        kt = k_ref[pl.ds(st, tq), :]
        if masked:
            kc = _idiv(lax.broadcasted_iota(jnp.int32, (tq, tq), 0), CHUNK)
            qc = _idiv(lax.broadcasted_iota(jnp.int32, (tq, tq), 1), CHUNK)
            keep = kc <= qc
        for c in range(2):
            s = _dot(kt, qt_sc[c])
            if masked:
                s = jnp.where(keep, s, -jnp.inf)
            s_sc[slot, c] = s
            mx_sc[slot, c] = jnp.max(s, axis=0, keepdims=True)

    def accumulate(t, slot):
        vt = vt_sc[t]
        for c in range(2):
            m_prev = m_sc[c]
            m_new = jnp.maximum(m_prev, mx_sc[slot, c])
            alpha = jnp.exp2(m_prev - m_new)
            p = jnp.exp2(s_sc[slot, c] - m_new)
            l_sc[c] = alpha * l_sc[c] + jnp.sum(p, axis=0, keepdims=True)
            acc_sc[c] = alpha * acc_sc[c] + _dot(vt, p.astype(BF16))
            m_sc[c] = m_new

    scores(qi, 0, True)

    def pair(u, carry):
        t0 = 2 * u
        accumulate(jnp.where(u == 0, qi, t0 - 1), 0)
        scores(t0, 1, False)
        accumulate(t0, 1)
        scores(t0 + 1, 0, False)
        return carry

    npair = qi // 2
    lax.fori_loop(0, npair, pair, 0)
    odd = qi % 2 == 1

    @pl.when(odd)
    def _():
        accumulate(jnp.where(qi == 1, qi, qi - 2), 0)
        scores(qi - 1, 1, False)
        accumulate(qi - 1, 1)

    @pl.when(jnp.logical_not(odd))
    def _():
        accumulate(jnp.where(qi == 0, qi, qi - 1), 0)

    lam =(jnp.exp(jnp.sum(lq1_ref[...] * lk1_ref[...], axis=-1, keepdims=True))
           - jnp.exp(jnp.sum(lq2_ref[...] * lk2_ref[...], axis=-1, keepdims=True)) + lambda_init)
    o = acc_sc[0] / l_sc[0] - lam * (acc_sc[1] / l_sc[1])
    ms = jnp.mean(o * o, axis=0, keepdims=True)
    on = (o * lax.rsqrt(ms + EPS)).T
    o_ref[...] = (on * gain_ref[...] * (1.0 - lambda_init)).astype(o_ref.dtype)


def _diff_attention_core(qkv, lq1, lk1, lq2, lk2, sub_gain, lambda_init, batch, seq):
    width = DA_HEADS * DA_VALUE_DIM
    qkv3 = qkv.reshape(batch, seq, 3 * width)
    tq = min(512, seq)
    kern = functools.partial(_attn_kernel, tq=tq, seq=seq, lambda_init=lambda_init)
    vec = pl.BlockSpec((1, DA_HEAD_DIM), lambda b, h, i: (0, 0))
    return pl.pallas_call(
        kern,
        out_shape=jax.ShapeDtypeStruct((batch, seq, width), BF16),
        grid=(batch, DA_HEADS, seq // tq),
        in_specs=[pl.BlockSpec((None, tq, DA_VALUE_DIM), lambda b, h, i: (b, i, h)),
                  pl.BlockSpec((None, seq, DA_VALUE_DIM), lambda b, h, i: (b, 0, DA_HEADS + h)),
                  pl.BlockSpec((None, seq, DA_VALUE_DIM), lambda b, h, i: (b, 0, 2 * DA_HEADS + h)),
                  vec, vec, vec, vec,
                  pl.BlockSpec((1, DA_VALUE_DIM), lambda b, h, i: (0, 0))],
        out_specs=pl.BlockSpec((None, tq, DA_VALUE_DIM), lambda b, h, i: (b, i, h)),
        scratch_shapes=[pltpu.VMEM((seq // tq, DA_VALUE_DIM, tq), BF16),
                        pltpu.VMEM((2, DA_VALUE_DIM, tq), BF16),
                        pltpu.VMEM((2, 2, tq, tq), F32),
                        pltpu.VMEM((2, 2, 1, tq), F32),
                        pltpu.VMEM((2, DA_VALUE_DIM, tq), F32),
                        pltpu.VMEM((2, 1, tq), F32),
                        pltpu.VMEM((2, 1, tq), F32)],
        compiler_params=_cparams(("parallel", "parallel", "arbitrary"), 48),
        name="diff_attention",
    )(qkv3, qkv3, qkv3, lq1.reshape(1, -1), lk1.reshape(1, -1), lq2.reshape(1, -1),
      lk2.reshape(1, -1), sub_gain.reshape(1, -1)).reshape(batch * seq, width)


def _ssd_kernel(xbc_ref, z_ref, dt_ref, cw_ref, cb_ref, dtb_ref, alog_ref, dsk_ref, ng_ref,
                tril_ref, e128_ref, e64_ref, y_ref, hbuf_sc, state_sc, *, steps_per_batch, inner):
    L = SSD_CHUNK
    G = SSM_GROUPS
    N = SSM_STATE
    gw = inner // G
    R = gw // SSM_HEAD_DIM

    @pl.when(pl.program_id(0) % steps_per_batch == 0)
    def _():
        hbuf_sc[0:SUBLANES, :] = jnp.zeros((SUBLANES, hbuf_sc.shape[1]), F32)
        state_sc[...] = jnp.zeros_like(state_sc)

    hbuf_sc[SUBLANES:SUBLANES + L, :] = xbc_ref[...]
    conv = cb_ref[...]
    for k in range(SSM_CONV):
        off = SUBLANES - (SSM_CONV - 1) + k
        conv = conv + cw_ref[k:k + 1, :] * hbuf_sc[off:off + L, :]
    hbuf_sc[0:SUBLANES, :] = hbuf_sc[L:L + SUBLANES, :]
    xbc = _silu(conv)

    x = dt_ref[...] + dtb_ref[...]
    dt = jnp.maximum(x, 0.0) + jnp.log(1.0 + jnp.exp(-jnp.abs(x)))
    adt = dt * (-jnp.exp(alog_ref[...]))

    acum = _dot3_lhs01(tril_ref[...], adt)
    acol128 = _dot3_rhs01(acum, e128_ref[...])
    acol64 = _dot3_rhs01(acum, e64_ref[...])
    dt64 = _dot3_rhs01(dt, e64_ref[...])

    ri = lax.broadcasted_iota(jnp.int32, acol128.shape, 0)
    ci = lax.broadcasted_iota(jnp.int32, acol128.shape, 1) & (L - 1)
    arow = jnp.sum(jnp.where(ri == ci, acol128, 0.0), axis=0, keepdims=True)
    decay_in = jnp.exp(jnp.where(ci <= ri, acol128 - arow, -jnp.inf))

    xs = xbc[:, :inner]
    xdt = xs * dt64
    last = acol64[L - 1:L, :]
    xdo = (xdt * jnp.exp(last - acol64)).astype(BF16)
    xdt = xdt.astype(BF16)
    e_a = jnp.exp(acol64)
    e_last = jnp.exp(last)

    rr = _idiv(lax.broadcasted_iota(jnp.int32, (R * L, gw), 0), L)
    cc = _idiv(lax.broadcasted_iota(jnp.int32, (R * L, gw), 1), SSM_HEAD_DIM)
    blockdiag = rr == cc

    for g in range(G):
        bg = xbc[:, inner + g * N: inner + (g + 1) * N]
        cg = xbc[:, inner + G * N + g * N: inner + G * N + (g + 1) * N].astype(BF16)
        bgt = bg.T.astype(BF16)
        cb = _dot(cg, bgt)
        cbt = jnp.concatenate([cb] * R, axis=1)
        mg = (cbt * decay_in[:, g * R * L:(g + 1) * R * L]).astype(BF16)
        xg = xdt[:, g * gw:(g + 1) * gw]
        xbd = jnp.where(blockdiag, jnp.concatenate([xg] * R, axis=0), jnp.zeros((), BF16))
        y = _dot(mg, xbd)
        st = state_sc[g]
        y = y + _dot(cg, st.astype(BF16)) * e_a[:, g * gw:(g + 1) * gw]
        state_sc[g] = st * e_last[:, g * gw:(g + 1) * gw] + _dot(bgt, xdo[:, g * gw:(g + 1) * gw])
        y = y + dsk_ref[:, g * gw:(g + 1) * gw] * xs[:, g * gw:(g + 1) * gw]
        y = y * _silu(z_ref[:, g * gw:(g + 1) * gw])
        ms = jnp.mean(y * y, axis=-1, keepdims=True)
        y_ref[:, g * gw:(g + 1) * gw] = (y * lax.rsqrt(ms + EPS) * ng_ref[:, g * gw:(g + 1) * gw]).astype(y_ref.dtype)


def _mamba2_mixer(x2, g, w_in, conv_w, conv_b, dt_bias, a_log, d_skip, norm_gain, w_out, seq):
    t, d = x2.shape
    nh = dt_bias.shape[0]
    inner = nh * SSM_HEAD_DIM
    conv_dim = conv_w.shape[1]
    L = SSD_CHUNK
    w_xz = jnp.concatenate([w_in[:, inner:inner + conv_dim], w_in[:, :inner]], axis=1).astype(BF16)
    w_dt = jnp.pad(w_in[:, inner + conv_dim:], ((0, 0), (0, LANES - nh))).astype(BF16)
    xz = _norm_matmul(x2, g, w_xz, F32, 1024)
    dtr = _norm_matmul(x2, g, w_dt, F32, LANES)

    pad1 = lambda v: jnp.pad(v, (0, LANES - nh)).reshape(1, LANES)
    rows = jnp.arange(L)
    tril = (rows[None, :] <= rows[:, None]).astype(BF16)
    head = jnp.arange(LANES)
    e128 = (head[:, None] == (jnp.arange(nh * L) // L)[None, :]).astype(BF16)
    e64 = (head[:, None] == (jnp.arange(inner) // SSM_HEAD_DIM)[None, :]).astype(BF16)
    dsk = jnp.repeat(d_skip, SSM_HEAD_DIM).reshape(1, inner)

    nblk_x = conv_dim // inner
    kern = functools.partial(_ssd_kernel, steps_per_batch=seq // L, inner=inner)
    const = lambda shape: _resident(shape, lambda i: (0, 0))
    y = pl.pallas_call(
        kern,
        out_shape=jax.ShapeDtypeStruct((t, inner), BF16),
        grid=(t // L,),
        in_specs=[pl.BlockSpec((L, conv_dim), lambda i: (i, 0)),
                  pl.BlockSpec((L, inner), lambda i: (i, nblk_x)),
                  pl.BlockSpec((L, LANES), lambda i: (i, 0)),
                  const((SSM_CONV, conv_dim)), const((1, conv_dim)),
                  const((1, LANES)), const((1, LANES)),
                  const((1, inner)), const((1, inner)),
                  const((L, L)), const((LANES, nh * L)), const((LANES, inner))],
        out_specs=pl.BlockSpec((L, inner), lambda i: (i, 0)),
        scratch_shapes=[pltpu.VMEM((L + SUBLANES, conv_dim), F32),
                        pltpu.VMEM((SSM_GROUPS, SSM_STATE, inner // SSM_GROUPS), F32)],
        compiler_params=_cparams(("arbitrary",), 48),
        name="ssd_scan",
    )(xz, xz, dtr, conv_w, conv_b.reshape(1, -1), pad1(dt_bias), pad1(a_log), dsk,
      norm_gain.reshape(1, -1), tril, e128, e64)
    return _matmul_residual(y, w_out.astype(BF16), x2)


def _s5_tables(a_re, a_im, log_dt, b_re, b_im, c_re, c_im, d_skip):
    L = S5_CHUNK
    G, P = a_re.shape
    K = b_re.shape[-1]
    lam = lax.complex(jnp.minimum(a_re, -1e-4), a_im)
    delta = jnp.exp(log_dt)[:, None]
    lam_bar = jnp.exp(lam * delta)
    b_bar = ((lam_bar - 1.0) / lam)[..., None] * lax.complex(b_re, b_im)
    c = lax.complex(c_re, c_im)
    tau = jnp.arange(L + 1, dtype=F32)
    pw = jnp.exp(tau[:, None, None] * (lam * delta)[None])

    kt = jnp.einsum("gkp,tgp,gpi->gtki", c, pw[:L], b_bar).real
    lag = jnp.arange(L)[None, :] - jnp.arange(L)[:, None]
    kg = jnp.where((lag >= 0)[None, :, :, None, None], kt[:, jnp.maximum(lag, 0)], 0.0)
    m = kg.transpose(0, 1, 4, 2, 3).reshape(G, L * K, L * K)

    bst = pw[:L][::-1].transpose(1, 0, 2)[:, :, None, :] * b_bar.transpose(0, 2, 1)[:, None, :, :]
    bst = bst.reshape(G, L * K, P)
    w = c.transpose(0, 2, 1)[:, :, None, :] * pw[1:].transpose(1, 2, 0)[:, :, :, None]
    w = w.reshape(G, P, L * K)

    par = (jnp.arange(G) % 2)[:, None, None, None] == jnp.arange(2)[None, None, :, None]
    put_cols = lambda v: jnp.where(par, v[:, :, None, :], 0.0).reshape(G, L * K, 2 * P)
    par_r = (jnp.arange(G) % 2)[:, None, None, None] == jnp.arange(2)[None, :, None, None]
    put_rows = lambda v: jnp.where(par_r, v[:, None, :, :], 0.0).reshape(G, 2 * P, L * K)
    lam_l = pw[L].reshape(G // 2, 1, 2 * P)
    dsk = jnp.tile(d_skip[:, None, :], (1, L, 1)).reshape(G, 1, L * K)
    return dict(
        m=m.astype(BF16),
        bst_re=put_cols(bst.real).astype(BF16), bst_im=put_cols(bst.imag).astype(BF16),
        cst_re=put_rows(w.real).astype(BF16), cst_im=put_rows(-w.imag).astype(BF16),
        lam_re=lam_l.real, lam_im=lam_l.imag, dsk=dsk)


def _s5_kernel(u_ref, m_ref, bre_ref, bim_ref, cre_ref, cim_ref, lre_ref, lim_ref, dsk_ref, o_ref,
               s_re_sc, s_im_sc, in_re_sc, in_im_sc, *, batch, rows_per_batch):
    s_re_sc[...] = _dot(u_ref[0], bre_ref[0]) + _dot(u_ref[1], bre_ref[1])
    s_im_sc[...] = _dot(u_ref[0], bim_ref[0]) + _dot(u_ref[1], bim_ref[1])
    lre = lre_ref[0]
    lim = lim_ref[0]

    for b in range(batch):
        base = b * rows_per_batch

        def body(c, carry):
            sre, sim = carry
            in_re_sc[pl.ds(base + c, 1), :] = sre
            in_im_sc[pl.ds(base + c, 1), :] = sim
            nre = lre * sre - lim * sim + s_re_sc[pl.ds(base + c, 1), :]
            nim = lre * sim + lim * sre + s_im_sc[pl.ds(base + c, 1), :]
            return nre, nim

        zero = jnp.zeros((1, lre.shape[-1]), F32)
        lax.fori_loop(0, rows_per_batch, body, (zero, zero))

    sre = in_re_sc[...].astype(BF16)
    sim = in_im_sc[...].astype(BF16)
    for gi in range(2):
        u = u_ref[gi]
        y = _dot(u, m_ref[gi]) + _dot(sre, cre_ref[gi]) + _dot(sim, cim_ref[gi])
        y = y + dsk_ref[gi] * u.astype(F32)
        inner = math.sqrt(2.0 / math.pi) * (y + 0.044715 * (y * y * y))
        o_ref[gi] = (0.5 * y * (1.0 + jnp.tanh(inner))).astype(o_ref.dtype)


def _glu_residual_kernel(a_ref, w_ref, r_ref, o_ref, *, n):
    a = a_ref[...]
    val = _dot(a, w_ref[:, :n])
    gate = _dot(a, w_ref[:, n:])
    o_ref[...] = r_ref[...] + val * _sigmoid(gate)


def _s5_mixer(x2, g, a_re, a_im, log_dt, b_re, b_im, c_re, c_im, d_skip, w_glu, batch, seq):
    t, d = x2.shape
    L = S5_CHUNK
    G, K = d_skip.shape
    tab = _s5_tables(a_re, a_im, log_dt, b_re, b_im, c_re, c_im, d_skip)
    u = _rmsnorm(x2, g, BF16)
    rows = t // L
    lk = L * K
    ug = u.reshape(rows, L, G, K).transpose(2, 0, 1, 3).reshape(G, rows, lk)
    kern = functools.partial(_s5_kernel, batch=batch, rows_per_batch=seq // L)
    pair = lambda shape: pl.BlockSpec(shape, lambda i: (i, 0, 0))
    p2 = 2 * S5_STATE
    ya = pl.pallas_call(
        kern,
        out_shape=jax.ShapeDtypeStruct((G, rows, lk), BF16),
        grid=(G // 2,),
        in_specs=[pair((2, rows, lk)), pair((2, lk, lk)),
                  pair((2, lk, p2)), pair((2, lk, p2)),
                  pair((2, p2, lk)), pair((2, p2, lk)),
                  pair((1, 1, p2)), pair((1, 1, p2)), pair((2, 1, lk))],
        out_specs=pair((2, rows, lk)),
        scratch_shapes=[pltpu.VMEM((rows, p2), F32)] * 4,
        compiler_params=_cparams(("parallel",), 40),
        name="s5_scan",
    )(ug, tab["m"], tab["bst_re"], tab["bst_im"], tab["cst_re"], tab["cst_im"],
      tab["lam_re"], tab["lam_im"], tab["dsk"])
    ya = ya.reshape(G, rows, L, K).transpose(1, 2, 0, 3).reshape(t, d)
    tm = min(512, t)
    return pl.pallas_call(
        functools.partial(_glu_residual_kernel, n=d),
        out_shape=jax.ShapeDtypeStruct((t, d), F32),
        grid=(t // tm,),
        in_specs=[pl.BlockSpec((tm, d), lambda i: (i, 0)),
                  _resident((d, 2 * d), lambda i: (0, 0)),
                  pl.BlockSpec((tm, d), lambda i: (i, 0))],
        out_specs=pl.BlockSpec((tm, d), lambda i: (i, 0)),
        compiler_params=_cparams(("parallel",), 40),
        name="glu_residual",
    )(ya, w_glu.astype(BF16), x2)


def kernel(x, norm_mix, norm_ffn, norm_final, attn_w_qkv, attn_w_o, attn_lambda_q1, attn_lambda_k1, attn_lambda_q2, attn_lambda_k2, attn_sub_gain, ssm_w_in, ssm_conv_w, ssm_conv_b, ssm_dt_bias, ssm_a_log, ssm_d, ssm_norm_gain, ssm_w_out, s5_a_re, s5_a_im, s5_log_dt, s5_b_re, s5_b_im, s5_c_re, s5_c_im, s5_d, s5_w_glu, ffn_w_up, ffn_conv_w, ffn_conv_b, ffn_w_down):
    batch, seq, d = x.shape
    depth = norm_mix.shape[0]
    x2 = x.reshape(batch * seq, d)
    for i in range(depth):
        kind = i % N_MIXERS
        j = i // N_MIXERS
        if kind == 0:
            lambda_init = 0.8 - 0.6 * math.exp(-0.3 * i)
            qkv = _qkv_proj(x2, norm_mix[i], attn_w_qkv[j], seq)
            o = _diff_attention_core(qkv, attn_lambda_q1[j], attn_lambda_k1[j], attn_lambda_q2[j],
                                     attn_lambda_k2[j], attn_sub_gain[j], lambda_init, batch, seq)
            x2 = _matmul_residual(o, attn_w_o[j].astype(BF16), x2)
        elif kind == 1:
            x2 = _mamba2_mixer(x2, norm_mix[i], ssm_w_in[j], ssm_conv_w[j], ssm_conv_b[j], ssm_dt_bias[j],
                               ssm_a_log[j], ssm_d[j], ssm_norm_gain[j], ssm_w_out[j], seq)
        else:
            x2 = _s5_mixer(x2, norm_mix[i], s5_a_re[j], s5_a_im[j], s5_log_dt[j], s5_b_re[j], s5_b_im[j],
                           s5_c_re[j], s5_c_im[j], s5_d[j], s5_w_glu[j], batch, seq)
        x2 = _conv_ffn(x2, norm_ffn[i], ffn_w_up[i], ffn_conv_w[i], ffn_conv_b[i], ffn_w_down[i], seq)
    return _rmsnorm(x2, norm_final, F32).reshape(batch, seq, d)
```

```python
import functools
import math

import jax
import jax.numpy as jnp
from jax import lax
from jax.experimental import pallas as pl
from jax.experimental.pallas import tpu as pltpu

F32 = jnp.float32
BF16 = jnp.bfloat16

EPS = 1e-6
N_MIXERS = 3
CHUNK = 64
DA_HEADS = 8
DA_HEAD_DIM = 64
DA_VALUE_DIM = 2 * DA_HEAD_DIM
ROPE_THETA = 500000.0
ROPE_DIM = DA_HEAD_DIM // 4
SSM_HEAD_DIM = 64
SSM_GROUPS = 8
SSM_STATE = 128
SSM_CONV = 4
SSD_CHUNK = 128
S5_GROUP = 16
S5_STATE = 64
S5_CHUNK = 32
ATTN_UNROLL = 4
FFN_CONV = 3
FFN_COLS = 256
LANES = 128
SUBLANES = 8
MIB = 1024 * 1024


def _cparams(sem, vmem_mib):
    return pltpu.CompilerParams(dimension_semantics=sem, vmem_limit_bytes=vmem_mib * MIB)


def _resident(shape, index_map):
    return pl.BlockSpec(shape, index_map, pipeline_mode=pl.Buffered(1))


def _rms_rows(x, g):
    ms = jnp.mean(x * x, axis=-1, keepdims=True)
    return x * lax.rsqrt(ms + EPS) * g


def _sigmoid(x):
    return 0.5 + 0.5 * jnp.tanh(0.5 * x)


def _silu(x):
    h = 0.5 * x
    return h + h * jnp.tanh(h)


def _idiv(v, n):
    return lax.shift_right_logical(v, int(math.log2(n)))


def _split3(x):
    hi = x.astype(BF16)
    r = x - hi.astype(F32)
    mid = r.astype(BF16)
    lo = (r - mid.astype(F32)).astype(BF16)
    return hi, mid, lo


def _dot(a, b):
    return jnp.dot(a, b, preferred_element_type=F32)


def _dot3_rhs01(x, e):
    hi, mid, lo = _split3(x)
    return _dot(hi, e) + _dot(mid, e) + _dot(lo, e)


def _dot3_lhs01(e, x):
    hi, mid, lo = _split3(x)
    return _dot(e, hi) + _dot(e, mid) + _dot(e, lo)


def _norm_kernel(x_ref, g_ref, o_ref):
    o_ref[...] = _rms_rows(x_ref[...], g_ref[...]).astype(o_ref.dtype)


def _rmsnorm(x2, g, out_dtype):
    t, d = x2.shape
    tm = min(512, t)
    return pl.pallas_call(
        _norm_kernel,
        out_shape=jax.ShapeDtypeStruct((t, d), out_dtype),
        grid=(t // tm,),
        in_specs=[pl.BlockSpec((tm, d), lambda i: (i, 0)),
                  pl.BlockSpec((1, d), lambda i: (0, 0))],
        out_specs=pl.BlockSpec((tm, d), lambda i: (i, 0)),
        compiler_params=_cparams(("parallel",), 32),
        name="rmsnorm",
    )(x2, g.reshape(1, d))


def _norm_matmul_kernel(x_ref, g_ref, w_ref, o_ref, hn_sc):
    @pl.when(pl.program_id(1) == 0)
    def _():
        hn_sc[...] = _rms_rows(x_ref[...], g_ref[...]).astype(BF16)

    o_ref[...] = _dot(hn_sc[...], w_ref[...]).astype(o_ref.dtype)


def _norm_matmul(x2, g, w, out_dtype, tn, tm):
    t, d = x2.shape
    n = w.shape[1]
    tm = min(tm, t)
    return pl.pallas_call(
        _norm_matmul_kernel,
        out_shape=jax.ShapeDtypeStruct((t, n), out_dtype),
        grid=(t // tm, n // tn),
        in_specs=[pl.BlockSpec((tm, d), lambda i, j: (i, 0)),
                  pl.BlockSpec((1, d), lambda i, j: (0, 0)),
                  pl.BlockSpec((d, tn), lambda i, j: (0, j))],
        out_specs=pl.BlockSpec((tm, tn), lambda i, j: (i, j)),
        scratch_shapes=[pltpu.VMEM((tm, d), BF16)],
        compiler_params=_cparams(("parallel", "arbitrary"), 40),
        name="norm_matmul",
    )(x2, g.reshape(1, d), w)


def _matmul_residual_kernel(a_ref, w_ref, r_ref, o_ref):
    o_ref[...] = r_ref[...] + _dot(a_ref[...], w_ref[...])


def _matmul_residual(a, w, res):
    t, k = a.shape
    n = w.shape[1]
    tm = min(512, t)
    return pl.pallas_call(
        _matmul_residual_kernel,
        out_shape=jax.ShapeDtypeStruct((t, n), F32),
        grid=(t // tm,),
        in_specs=[pl.BlockSpec((tm, k), lambda i: (i, 0)),
                  _resident((k, n), lambda i: (0, 0)),
                  pl.BlockSpec((tm, n), lambda i: (i, 0))],
        out_specs=pl.BlockSpec((tm, n), lambda i: (i, 0)),
        compiler_params=_cparams(("parallel",), 40),
        name="matmul_residual",
    )(a, w, res)


def _ffn_kernel(x_ref, g_ref, wup_ref, cw_ref, cb_ref, wdn_ref, o_ref,
                hn_sc, hbuf_sc, halo_sc, *, tm, ffn, steps_per_batch):
    cols = FFN_COLS
    nchunk = ffn // cols

    @pl.when(pl.program_id(0) % steps_per_batch == 0)
    def _():
        halo_sc[...] = jnp.zeros_like(halo_sc)

    x = x_ref[...]
    hn_sc[...] = _rms_rows(x, g_ref[...]).astype(BF16)
    o_ref[...] = x

    def up(c):
        lo = c * cols
        buf = hbuf_sc.at[c % 2]
        hn = hn_sc[...]
        buf[0:SUBLANES, :] = halo_sc[c]
        buf[SUBLANES:SUBLANES + tm, 0:cols] = _dot(hn, wup_ref[:, lo:lo + cols])
        buf[SUBLANES:SUBLANES + tm, cols:2 * cols] = _dot(hn, wup_ref[:, ffn + lo:ffn + lo + cols])
        halo_sc[c] = buf[tm:tm + SUBLANES, :]

    def down(c):
        lo = c * cols
        buf = hbuf_sc.at[c % 2]
        w3 = jnp.concatenate([cw_ref[:, lo:lo + cols], cw_ref[:, ffn + lo:ffn + lo + cols]], axis=1)
        y = jnp.concatenate([cb_ref[:, lo:lo + cols], cb_ref[:, ffn + lo:ffn + lo + cols]], axis=1)
        for k in range(FFN_CONV):
            off = SUBLANES - (FFN_CONV - 1) + k
            y = y + w3[k:k + 1, :] * buf[off:off + tm, :]
        act = (_silu(y[:, :cols]) * y[:, cols:]).astype(BF16)
        o_ref[...] += _dot(act, wdn_ref[lo:lo + cols, :])

    up(0)
    for c in range(nchunk):
        if c + 1 < nchunk:
            up(c + 1)
        down(c)


def _conv_ffn(x2, g, w_up, conv_w, conv_b, w_down, seq):
    t, d = x2.shape
    ffn = w_down.shape[0]
    tm = min(512, seq)
    kern = functools.partial(_ffn_kernel, tm=tm, ffn=ffn, steps_per_batch=seq // tm)
    return pl.pallas_call(
        kern,
        out_shape=jax.ShapeDtypeStruct((t, d), F32),
        grid=(t // tm,),
        in_specs=[pl.BlockSpec((tm, d), lambda i: (i, 0)),
                  _resident((1, d), lambda i: (0, 0)),
                  _resident((d, 2 * ffn), lambda i: (0, 0)),
                  _resident((FFN_CONV, 2 * ffn), lambda i: (0, 0)),
                  _resident((1, 2 * ffn), lambda i: (0, 0)),
                  _resident((ffn, d), lambda i: (0, 0))],
        out_specs=pl.BlockSpec((tm, d), lambda i: (i, 0)),
        scratch_shapes=[pltpu.VMEM((tm, d), BF16),
                        pltpu.VMEM((2, tm + SUBLANES, 2 * FFN_COLS), F32),
                        pltpu.VMEM((ffn // FFN_COLS, SUBLANES, 2 * FFN_COLS), F32)],
        compiler_params=_cparams(("arbitrary",), 48),
        name="conv_ffn",
    )(x2, g.reshape(1, d), w_up.astype(BF16), conv_w, conv_b.reshape(1, -1), w_down.astype(BF16))


def _qkv_kernel(x_ref, g_ref, w_ref, wr_ref, cos_ref, sin_ref, o_ref, hn_sc, *, tn, n_rope_tiles):
    j = pl.program_id(1)

    @pl.when(j == 0)
    def _():
        hn_sc[...] = _rms_rows(x_ref[...], g_ref[...]).astype(BF16)

    @pl.when(j < n_rope_tiles)
    def _():
        hn = hn_sc[...]
        reps = tn // LANES
        cos = jnp.concatenate([cos_ref[...]] * reps, axis=1)
        sin = jnp.concatenate([sin_ref[...]] * reps, axis=1)
        o_ref[...] = (_dot(hn, w_ref[...]) * cos + _dot(hn, wr_ref[...]) * sin).astype(o_ref.dtype)

    @pl.when(j >= n_rope_tiles)
    def _():
        o_ref[...] = _dot(hn_sc[...], w_ref[...]).astype(o_ref.dtype)


def _rope_tables(seq):
    half = ROPE_DIM // 2
    pos = jnp.arange(seq, dtype=F32)
    inv_freq = ROPE_THETA ** (-jnp.arange(0, ROPE_DIM, 2, dtype=F32) / ROPE_DIM)
    ang = pos[:, None] * inv_freq[None, :]
    cos, sin = jnp.cos(ang), jnp.sin(ang)
    lane = jnp.arange(LANES) % DA_HEAD_DIM
    idx = lane % half
    cos_l = jnp.where(lane[None, :] < ROPE_DIM, cos[:, idx], 1.0)
    sin_l = jnp.where(lane[None, :] < ROPE_DIM, sin[:, idx], 0.0)
    return cos_l, sin_l


def _rope_partner_weight(w):
    half = ROPE_DIM // 2
    col = jnp.arange(w.shape[1])
    lane = col % DA_HEAD_DIM
    partner = jnp.where(lane < half, col + half, col - half)
    sign = jnp.where(lane < half, -1.0, jnp.where(lane < ROPE_DIM, 1.0, 0.0))
    return w[:, jnp.clip(partner, 0, w.shape[1] - 1)] * sign[None, :]


def _qkv_proj(x2, g, w_qkv, seq):
    t, d = x2.shape
    width = w_qkv.shape[1] // 3
    scale = DA_HEAD_DIM ** -0.5 * math.log2(math.e)
    w = jnp.concatenate([w_qkv[:, :width] * scale, w_qkv[:, width:]], axis=1).astype(BF16)
    wr = _rope_partner_weight(w[:, :2 * width])
    n = w.shape[1]
    tm = min(512, seq)
    tn = 512
    cos_l, sin_l = _rope_tables(seq)
    spb = seq // tm
    n_rope_tiles = 2 * width // tn
    kern = functools.partial(_qkv_kernel, tn=tn, n_rope_tiles=n_rope_tiles)
    tab = pl.BlockSpec((tm, LANES), lambda i, j: (i % spb, 0))
    return pl.pallas_call(
        kern,
        out_shape=jax.ShapeDtypeStruct((t, n), BF16),
        grid=(t // tm, n // tn),
        in_specs=[pl.BlockSpec((tm, d), lambda i, j: (i, 0)),
                  pl.BlockSpec((1, d), lambda i, j: (0, 0)),
                  pl.BlockSpec((d, tn), lambda i, j: (0, j)),
                  pl.BlockSpec((d, tn), lambda i, j: (0, jnp.minimum(j, n_rope_tiles - 1))),
                  tab, tab],
        out_specs=pl.BlockSpec((tm, tn), lambda i, j: (i, j)),
        scratch_shapes=[pltpu.VMEM((tm, d), BF16)],
        compiler_params=_cparams(("parallel", "arbitrary"), 40),
        name="qkv_rope",
    )(x2, g.reshape(1, d), w, wr, cos_l, sin_l)


def _attn_kernel(q_ref, k_ref, v_ref, lq1_ref, lk1_ref, lq2_ref, lk2_ref, gain_ref, o_ref,
                 vt_sc, qt_sc, s_sc, mx_sc, acc_sc, m_sc, l_sc, *, tq, seq, lambda_init):
    qi = pl.program_id(2)
    nkv = seq // tq

    @pl.when(qi == 0)
    def _():
        def tr(t, carry):
            st = pl.multiple_of(t * tq, tq)
            vt_sc[t] = v_ref[pl.ds(st, tq), :].astype(F32).T.astype(BF16)
            return carry
        lax.fori_loop(0, nkv, tr, 0)

    qt = q_ref[...].astype(F32).T
    row = lax.broadcasted_iota(jnp.int32, qt.shape, 0)
    qt_sc[0] = jnp.where(row < DA_HEAD_DIM, qt, 0.0).astype(BF16)
    qt_sc[1] = jnp.where(row >= DA_HEAD_DIM, qt, 0.0).astype(BF16)

    m_sc[...] = jnp.full_like(m_sc, -jnp.inf)
    l_sc[...] = jnp.zeros_like(l_sc)
    acc_sc[...] = jnp.zeros_like(acc_sc)

    def scores(t, slot, masked):
        st = pl.multiple_of(t * tq, tq)
        kt = k_ref[pl.ds(st, tq), :]
        if masked:
            kc = _idiv(lax.broadcasted_iota(jnp.int32, (tq, tq), 0), CHUNK)
            qc = _idiv(lax.broadcasted_iota(jnp.int32, (tq, tq), 1), CHUNK)
            keep = kc <= qc
        for c in range(2):
            s = _dot(kt, qt_sc[c])
            if masked:
                s = jnp.where(keep, s, -jnp.inf)
            s_sc[slot, c] = s
            mx_sc[slot, c] = jnp.max(s, axis=0, keepdims=True)

    def accumulate(t, slot):
        vt = vt_sc[t]
        for c in range(2):
            m_prev = m_sc[c]
            m_new = jnp.maximum(m_prev, mx_sc[slot, c])
            alpha = jnp.exp2(m_prev - m_new)
            p = jnp.exp2(s_sc[slot, c] - m_new)
            l_sc[c] = alpha * l_sc[c] + jnp.sum(p, axis=0, keepdims=True)
            acc_sc[c] = alpha * acc_sc[c] + _dot(vt, p.astype(BF16))
            m_sc[c] = m_new

    scores(qi, 0, True)

    def run(width):
        def body(u, carry):
            for j in range(width):
                t = width * u + j
                scores(t, (j + 1) % 2, False)
                accumulate(jnp.where(t == 0, qi, t - 1), j % 2)
            return carry
        return body

    nquad = qi // ATTN_UNROLL
    npair = qi // 2
    lax.fori_loop(0, nquad, run(ATTN_UNROLL), 0)
    lax.fori_loop(nquad * (ATTN_UNROLL // 2), npair, run(2), 0)
    odd = qi % 2 == 1

    @pl.when(odd)
    def _():
        scores(qi - 1, 1, False)
        accumulate(jnp.where(qi == 1, qi, qi - 2), 0)
        accumulate(qi - 1, 1)

    @pl.when(jnp.logical_not(odd))
    def _():
        accumulate(jnp.where(qi == 0, qi, qi - 1), 0)

    lam =(jnp.exp(jnp.sum(lq1_ref[...] * lk1_ref[...], axis=-1, keepdims=True))
           - jnp.exp(jnp.sum(lq2_ref[...] * lk2_ref[...], axis=-1, keepdims=True)) + lambda_init)
    o = acc_sc[0] / l_sc[0] - lam * (acc_sc[1] / l_sc[1])
    ms = jnp.mean(o * o, axis=0, keepdims=True)
    on = (o * lax.rsqrt(ms + EPS)).T
    o_ref[...] = (on * gain_ref[...] * (1.0 - lambda_init)).astype(o_ref.dtype)


def _diff_attention_core(qkv, lq1, lk1, lq2, lk2, sub_gain, lambda_init, batch, seq):
    width = DA_HEADS * DA_VALUE_DIM
    qkv3 = qkv.reshape(batch, seq, 3 * width)
    tq = min(512, seq)
    kern = functools.partial(_attn_kernel, tq=tq, seq=seq, lambda_init=lambda_init)
    vec = pl.BlockSpec((1, DA_HEAD_DIM), lambda b, h, i: (0, 0))
    return pl.pallas_call(
        kern,
        out_shape=jax.ShapeDtypeStruct((batch, seq, width), BF16),
        grid=(batch, DA_HEADS, seq // tq),
        in_specs=[pl.BlockSpec((None, tq, DA_VALUE_DIM), lambda b, h, i: (b, i, h)),
                  pl.BlockSpec((None, seq, DA_VALUE_DIM), lambda b, h, i: (b, 0, DA_HEADS + h)),
                  pl.BlockSpec((None, seq, DA_VALUE_DIM), lambda b, h, i: (b, 0, 2 * DA_HEADS + h)),
                  vec, vec, vec, vec,
                  pl.BlockSpec((1, DA_VALUE_DIM), lambda b, h, i: (0, 0))],
        out_specs=pl.BlockSpec((None, tq, DA_VALUE_DIM), lambda b, h, i: (b, i, h)),
        scratch_shapes=[pltpu.VMEM((seq // tq, DA_VALUE_DIM, tq), BF16),
                        pltpu.VMEM((2, DA_VALUE_DIM, tq), BF16),
                        pltpu.VMEM((2, 2, tq, tq), F32),
                        pltpu.VMEM((2, 2, 1, tq), F32),
                        pltpu.VMEM((2, DA_VALUE_DIM, tq), F32),
                        pltpu.VMEM((2, 1, tq), F32),
                        pltpu.VMEM((2, 1, tq), F32)],
        compiler_params=_cparams(("parallel", "parallel", "arbitrary"), 48),
        name="diff_attention",
    )(qkv3, qkv3, qkv3, lq1.reshape(1, -1), lk1.reshape(1, -1), lq2.reshape(1, -1),
      lk2.reshape(1, -1), sub_gain.reshape(1, -1)).reshape(batch * seq, width)


def _ssd_kernel(xbc_ref, z_ref, dt_ref, cw_ref, cb_ref, dtb_ref, alog_ref, dsk_ref, ng_ref,
                tril_ref, e128_ref, e64_ref, y_ref, hbuf_sc, state_sc, *, steps_per_batch, inner):
    L = SSD_CHUNK
    G = SSM_GROUPS
    N = SSM_STATE
    gw = inner // G
    R = gw // SSM_HEAD_DIM

    @pl.when(pl.program_id(0) % steps_per_batch == 0)
    def _():
        hbuf_sc[0:SUBLANES, :] = jnp.zeros((SUBLANES, hbuf_sc.shape[1]), F32)
        state_sc[...] = jnp.zeros_like(state_sc)

    hbuf_sc[SUBLANES:SUBLANES + L, :] = xbc_ref[...].astype(F32)
    conv = cb_ref[...]
    for k in range(SSM_CONV):
        off = SUBLANES - (SSM_CONV - 1) + k
        conv = conv + cw_ref[k:k + 1, :] * hbuf_sc[off:off + L, :]
    hbuf_sc[0:SUBLANES, :] = hbuf_sc[L:L + SUBLANES, :]
    xbc = _silu(conv)

    x = dt_ref[...] + dtb_ref[...]
    dt = jnp.maximum(x, 0.0) + jnp.log(1.0 + jnp.exp(-jnp.abs(x)))
    adt = dt * (-jnp.exp(alog_ref[...]))

    acum = _dot3_lhs01(tril_ref[...], adt)
    acol128 = _dot3_rhs01(acum, e128_ref[...])
    acol64 = _dot3_rhs01(acum, e64_ref[...])
    dt64 = _dot3_rhs01(dt, e64_ref[...])

    ri = lax.broadcasted_iota(jnp.int32, acol128.shape, 0)
    ci = lax.broadcasted_iota(jnp.int32, acol128.shape, 1) & (L - 1)
    arow = jnp.sum(jnp.where(ri == ci, acol128, 0.0), axis=0, keepdims=True)
    decay_in = jnp.exp(jnp.where(ci <= ri, acol128 - arow, -jnp.inf))

    xs = xbc[:, :inner]
    xdt = xs * dt64
    last = acol64[L - 1:L, :]
    xdo = (xdt * jnp.exp(last - acol64)).astype(BF16)
    xdt = xdt.astype(BF16)
    e_a = jnp.exp(acol64)
    e_last = jnp.exp(last)

    rr = _idiv(lax.broadcasted_iota(jnp.int32, (R * L, gw), 0), L)
    cc = _idiv(lax.broadcasted_iota(jnp.int32, (R * L, gw), 1), SSM_HEAD_DIM)
    blockdiag = rr == cc

    for g in range(G):
        bg = xbc[:, inner + g * N: inner + (g + 1) * N]
        cg = xbc[:, inner + G * N + g * N: inner + G * N + (g + 1) * N].astype(BF16)
        bgt = bg.T.astype(BF16)
        cb = _dot(cg, bgt)
        cbt = jnp.concatenate([cb] * R, axis=1)
        mg = (cbt * decay_in[:, g * R * L:(g + 1) * R * L]).astype(BF16)
        xg = xdt[:, g * gw:(g + 1) * gw]
        xbd = jnp.where(blockdiag, jnp.concatenate([xg] * R, axis=0), jnp.zeros((), BF16))
        y = _dot(mg, xbd)
        st = state_sc[g]
        y = y + _dot(cg, st.astype(BF16)) * e_a[:, g * gw:(g + 1) * gw]
        state_sc[g] = st * e_last[:, g * gw:(g + 1) * gw] + _dot(bgt, xdo[:, g * gw:(g + 1) * gw])
        y = y + dsk_ref[:, g * gw:(g + 1) * gw] * xs[:, g * gw:(g + 1) * gw]
        y = y * _silu(z_ref[:, g * gw:(g + 1) * gw].astype(F32))
        ms = jnp.mean(y * y, axis=-1, keepdims=True)
        y_ref[:, g * gw:(g + 1) * gw] = (y * lax.rsqrt(ms + EPS) * ng_ref[:, g * gw:(g + 1) * gw]).astype(y_ref.dtype)


def _mamba2_mixer(x2, g, w_in, conv_w, conv_b, dt_bias, a_log, d_skip, norm_gain, w_out, seq):
    t, d = x2.shape
    nh = dt_bias.shape[0]
    inner = nh * SSM_HEAD_DIM
    conv_dim = conv_w.shape[1]
    L = SSD_CHUNK
    w_xz = jnp.concatenate([w_in[:, inner:inner + conv_dim], w_in[:, :inner]], axis=1).astype(BF16)
    w_dt = jnp.pad(w_in[:, inner + conv_dim:], ((0, 0), (0, LANES - nh))).astype(BF16)
    xz = _norm_matmul(x2, g, w_xz, BF16, 1024, 1024)
    dtr = _norm_matmul(x2, g, w_dt, F32, LANES, 512)

    pad1 = lambda v: jnp.pad(v, (0, LANES - nh)).reshape(1, LANES)
    rows = jnp.arange(L)
    tril = (rows[None, :] <= rows[:, None]).astype(BF16)
    head = jnp.arange(LANES)
    e128 = (head[:, None] == (jnp.arange(nh * L) // L)[None, :]).astype(BF16)
    e64 = (head[:, None] == (jnp.arange(inner) // SSM_HEAD_DIM)[None, :]).astype(BF16)
    dsk = jnp.repeat(d_skip, SSM_HEAD_DIM).reshape(1, inner)

    nblk_x = conv_dim // inner
    kern = functools.partial(_ssd_kernel, steps_per_batch=seq // L, inner=inner)
    const = lambda shape: _resident(shape, lambda i: (0, 0))
    y = pl.pallas_call(
        kern,
        out_shape=jax.ShapeDtypeStruct((t, inner), BF16),
        grid=(t // L,),
        in_specs=[pl.BlockSpec((L, conv_dim), lambda i: (i, 0)),
                  pl.BlockSpec((L, inner), lambda i: (i, nblk_x)),
                  pl.BlockSpec((L, LANES), lambda i: (i, 0)),
                  const((SSM_CONV, conv_dim)), const((1, conv_dim)),
                  const((1, LANES)), const((1, LANES)),
                  const((1, inner)), const((1, inner)),
                  const((L, L)), const((LANES, nh * L)), const((LANES, inner))],
        out_specs=pl.BlockSpec((L, inner), lambda i: (i, 0)),
        scratch_shapes=[pltpu.VMEM((L + SUBLANES, conv_dim), F32),
                        pltpu.VMEM((SSM_GROUPS, SSM_STATE, inner // SSM_GROUPS), F32)],
        compiler_params=_cparams(("arbitrary",), 48),
        name="ssd_scan",
    )(xz, xz, dtr, conv_w, conv_b.reshape(1, -1), pad1(dt_bias), pad1(a_log), dsk,
      norm_gain.reshape(1, -1), tril, e128, e64)
    return _matmul_residual(y, w_out.astype(BF16), x2)


def _s5_tables(a_re, a_im, log_dt, b_re, b_im, c_re, c_im, d_skip):
    L = S5_CHUNK
    G, P = a_re.shape
    K = b_re.shape[-1]
    lam = lax.complex(jnp.minimum(a_re, -1e-4), a_im)
    delta = jnp.exp(log_dt)[:, None]
    lam_bar = jnp.exp(lam * delta)
    b_bar = ((lam_bar - 1.0) / lam)[..., None] * lax.complex(b_re, b_im)
    c = lax.complex(c_re, c_im)
    tau = jnp.arange(L + 1, dtype=F32)
    pw = jnp.exp(tau[:, None, None] * (lam * delta)[None])

    kt = jnp.einsum("gkp,tgp,gpi->gtki", c, pw[:L], b_bar).real
    lag = jnp.arange(L)[None, :] - jnp.arange(L)[:, None]
    kg = jnp.where((lag >= 0)[None, :, :, None, None], kt[:, jnp.maximum(lag, 0)], 0.0)
    m = kg.transpose(0, 1, 4, 2, 3).reshape(G, L * K, L * K)

    bst = pw[:L][::-1].transpose(1, 0, 2)[:, :, None, :] * b_bar.transpose(0, 2, 1)[:, None, :, :]
    bst = bst.reshape(G, L * K, P)
    w = c.transpose(0, 2, 1)[:, :, None, :] * pw[1:].transpose(1, 2, 0)[:, :, :, None]
    w = w.reshape(G, P, L * K)

    par = (jnp.arange(G) % 2)[:, None, None, None] == jnp.arange(2)[None, None, :, None]
    put_cols = lambda v: jnp.where(par, v[:, :, None, :], 0.0).reshape(G, L * K, 2 * P)
    par_r = (jnp.arange(G) % 2)[:, None, None, None] == jnp.arange(2)[None, :, None, None]
    put_rows = lambda v: jnp.where(par_r, v[:, None, :, :], 0.0).reshape(G, 2 * P, L * K)
    lam_l = pw[L].reshape(G // 2, 1, 2 * P)
    dsk = jnp.tile(d_skip[:, None, :], (1, L, 1)).reshape(G, 1, L * K)
    return dict(
        m=m.astype(BF16),
        bst_re=put_cols(bst.real).astype(BF16), bst_im=put_cols(bst.imag).astype(BF16),
        cst_re=put_rows(w.real).astype(BF16), cst_im=put_rows(-w.imag).astype(BF16),
        lam_re=lam_l.real, lam_im=lam_l.imag, dsk=dsk)


def _s5_kernel(u_ref, m_ref, bre_ref, bim_ref, cre_ref, cim_ref, lre_ref, lim_ref, dsk_ref, o_ref,
               s_re_sc, s_im_sc, in_re_sc, in_im_sc, *, batch, rows_per_batch):
    s_re_sc[...] = _dot(u_ref[0], bre_ref[0]) + _dot(u_ref[1], bre_ref[1])
    s_im_sc[...] = _dot(u_ref[0], bim_ref[0]) + _dot(u_ref[1], bim_ref[1])
    lre = lre_ref[0]
    lim = lim_ref[0]

    for b in range(batch):
        base = b * rows_per_batch

        def body(c, carry):
            sre, sim = carry
            in_re_sc[pl.ds(base + c, 1), :] = sre
            in_im_sc[pl.ds(base + c, 1), :] = sim
            nre = lre * sre - lim * sim + s_re_sc[pl.ds(base + c, 1), :]
            nim = lre * sim + lim * sre + s_im_sc[pl.ds(base + c, 1), :]
            return nre, nim

        zero = jnp.zeros((1, lre.shape[-1]), F32)
        lax.fori_loop(0, rows_per_batch, body, (zero, zero))

    sre = in_re_sc[...].astype(BF16)
    sim = in_im_sc[...].astype(BF16)
    for gi in range(2):
        u = u_ref[gi]
        y = _dot(u, m_ref[gi]) + _dot(sre, cre_ref[gi]) + _dot(sim, cim_ref[gi])
        y = y + dsk_ref[gi] * u.astype(F32)
        inner = math.sqrt(2.0 / math.pi) * (y + 0.044715 * (y * y * y))
        o_ref[gi] = (0.5 * y * (1.0 + jnp.tanh(inner))).astype(o_ref.dtype)


def _glu_residual_kernel(a_ref, w_ref, r_ref, o_ref, *, n):
    a = a_ref[...]
    val = _dot(a, w_ref[:, :n])
    gate = _dot(a, w_ref[:, n:])
    o_ref[...] = r_ref[...] + val * _sigmoid(gate)


def _s5_mixer(x2, g, a_re, a_im, log_dt, b_re, b_im, c_re, c_im, d_skip, w_glu, batch, seq):
    t, d = x2.shape
    L = S5_CHUNK
    G, K = d_skip.shape
    tab = _s5_tables(a_re, a_im, log_dt, b_re, b_im, c_re, c_im, d_skip)
    u = _rmsnorm(x2, g, BF16)
    rows = t // L
    lk = L * K
    ug = u.reshape(rows, L, G, K).transpose(2, 0, 1, 3).reshape(G, rows, lk)
    kern = functools.partial(_s5_kernel, batch=batch, rows_per_batch=seq // L)
    pair = lambda shape: pl.BlockSpec(shape, lambda i: (i, 0, 0))
    p2 = 2 * S5_STATE
    ya = pl.pallas_call(
        kern,
        out_shape=jax.ShapeDtypeStruct((G, rows, lk), BF16),
        grid=(G // 2,),
        in_specs=[pair((2, rows, lk)), pair((2, lk, lk)),
                  pair((2, lk, p2)), pair((2, lk, p2)),
                  pair((2, p2, lk)), pair((2, p2, lk)),
                  pair((1, 1, p2)), pair((1, 1, p2)), pair((2, 1, lk))],
        out_specs=pair((2, rows, lk)),
        scratch_shapes=[pltpu.VMEM((rows, p2), F32)] * 4,
        compiler_params=_cparams(("parallel",), 40),
        name="s5_scan",
    )(ug, tab["m"], tab["bst_re"], tab["bst_im"], tab["cst_re"], tab["cst_im"],
      tab["lam_re"], tab["lam_im"], tab["dsk"])
    ya = ya.reshape(G, rows, L, K).transpose(1, 2, 0, 3).reshape(t, d)
    tm = min(512, t)
    return pl.pallas_call(
        functools.partial(_glu_residual_kernel, n=d),
        out_shape=jax.ShapeDtypeStruct((t, d), F32),
        grid=(t // tm,),
        in_specs=[pl.BlockSpec((tm, d), lambda i: (i, 0)),
                  _resident((d, 2 * d), lambda i: (0, 0)),
                  pl.BlockSpec((tm, d), lambda i: (i, 0))],
        out_specs=pl.BlockSpec((tm, d), lambda i: (i, 0)),
        compiler_params=_cparams(("parallel",), 40),
        name="glu_residual",
    )(ya, w_glu.astype(BF16), x2)


def kernel(x, norm_mix, norm_ffn, norm_final, attn_w_qkv, attn_w_o, attn_lambda_q1, attn_lambda_k1, attn_lambda_q2, attn_lambda_k2, attn_sub_gain, ssm_w_in, ssm_conv_w, ssm_conv_b, ssm_dt_bias, ssm_a_log, ssm_d, ssm_norm_gain, ssm_w_out, s5_a_re, s5_a_im, s5_log_dt, s5_b_re, s5_b_im, s5_c_re, s5_c_im, s5_d, s5_w_glu, ffn_w_up, ffn_conv_w, ffn_conv_b, ffn_w_down):
    batch, seq, d = x.shape
    depth = norm_mix.shape[0]
    x2 = x.reshape(batch * seq, d)
    for i in range(depth):
        kind = i % N_MIXERS
        j = i // N_MIXERS
        if kind == 0:
            lambda_init = 0.8 - 0.6 * math.exp(-0.3 * i)
            qkv = _qkv_proj(x2, norm_mix[i], attn_w_qkv[j], seq)
            o = _diff_attention_core(qkv, attn_lambda_q1[j], attn_lambda_k1[j], attn_lambda_q2[j],
                                     attn_lambda_k2[j], attn_sub_gain[j], lambda_init, batch, seq)
            x2 = _matmul_residual(o, attn_w_o[j].astype(BF16), x2)
        elif kind == 1:
            x2 = _mamba2_mixer(x2, norm_mix[i], ssm_w_in[j], ssm_conv_w[j], ssm_conv_b[j], ssm_dt_bias[j],
                               ssm_a_log[j], ssm_d[j], ssm_norm_gain[j], ssm_w_out[j], seq)
        else:
            x2 = _s5_mixer(x2, norm_mix[i], s5_a_re[j], s5_a_im[j], s5_log_dt[j], s5_b_re[j], s5_b_im[j],
                           s5_c_re[j], s5_c_im[j], s5_d[j], s5_w_glu[j], batch, seq)
        x2 = _conv_ffn(x2, norm_ffn[i], ffn_w_up[i], ffn_conv_w[i], ffn_conv_b[i], ffn_w_down[i], seq)
    return _rmsnorm(x2, norm_final, F32).reshape(batch, seq, d)
```

```python
import functools
import math

import jax
import jax.numpy as jnp
from jax import lax
from jax.experimental import pallas as pl
from jax.experimental.pallas import tpu as pltpu

F32 = jnp.float32
BF16 = jnp.bfloat16

EPS = 1e-6
N_MIXERS = 3
CHUNK = 64
DA_HEADS = 8
DA_HEAD_DIM = 64
DA_VALUE_DIM = 2 * DA_HEAD_DIM
ROPE_THETA = 500000.0
ROPE_DIM = DA_HEAD_DIM // 4
SSM_HEAD_DIM = 64
SSM_GROUPS = 8
SSM_STATE = 128
SSM_CONV = 4
SSD_CHUNK = 128
S5_GROUP = 16
S5_STATE = 64
S5_CHUNK = 32
ATTN_UNROLL = 4
FFN_CONV = 3
FFN_COLS = 256
FFN_ROWS = 512
LANES = 128
SUBLANES = 8
MIB = 1024 * 1024


def _cparams(sem, vmem_mib):
    return pltpu.CompilerParams(dimension_semantics=sem, vmem_limit_bytes=vmem_mib * MIB)


def _resident(shape, index_map):
    return pl.BlockSpec(shape, index_map, pipeline_mode=pl.Buffered(1))


def _rms_rows(x, g):
    ms = jnp.mean(x * x, axis=-1, keepdims=True)
    return x * lax.rsqrt(ms + EPS) * g


def _sigmoid(x):
    return 0.5 + 0.5 * jnp.tanh(0.5 * x)


def _silu(x):
    h = 0.5 * x
    return h + h * jnp.tanh(h)


def _idiv(v, n):
    return lax.shift_right_logical(v, int(math.log2(n)))


def _split3(x):
    hi = x.astype(BF16)
    r = x - hi.astype(F32)
    mid = r.astype(BF16)
    lo = (r - mid.astype(F32)).astype(BF16)
    return hi, mid, lo


def _dot(a, b):
    return jnp.dot(a, b, preferred_element_type=F32)


def _dot3_rhs01(x, e):
    hi, mid, lo = _split3(x)
    return _dot(hi, e) + _dot(mid, e) + _dot(lo, e)


def _dot3_lhs01(e, x):
    hi, mid, lo = _split3(x)
    return _dot(e, hi) + _dot(e, mid) + _dot(e, lo)


def _norm_kernel(x_ref, g_ref, o_ref):
    o_ref[...] = _rms_rows(x_ref[...], g_ref[...]).astype(o_ref.dtype)


def _rmsnorm(x2, g, out_dtype):
    t, d = x2.shape
    tm = min(512, t)
    return pl.pallas_call(
        _norm_kernel,
        out_shape=jax.ShapeDtypeStruct((t, d), out_dtype),
        grid=(t // tm,),
        in_specs=[pl.BlockSpec((tm, d), lambda i: (i, 0)),
                  pl.BlockSpec((1, d), lambda i: (0, 0))],
        out_specs=pl.BlockSpec((tm, d), lambda i: (i, 0)),
        compiler_params=_cparams(("parallel",), 32),
        name="rmsnorm",
    )(x2, g.reshape(1, d))


def _norm_matmul_kernel(x_ref, g_ref, w_ref, o_ref, hn_sc, *, tn):
    hn_sc[...] = _rms_rows(x_ref[...], g_ref[...]).astype(BF16)
    for j in range(w_ref.shape[1] // tn):
        o_ref[:, j * tn:(j + 1) * tn] = _dot(hn_sc[...], w_ref[:, j * tn:(j + 1) * tn]).astype(o_ref.dtype)


def _norm_matmul(x2, g, w, out_dtype, tn, tm):
    t, d = x2.shape
    n = w.shape[1]
    tm = min(tm, t)
    return pl.pallas_call(
        functools.partial(_norm_matmul_kernel, tn=tn),
        out_shape=jax.ShapeDtypeStruct((t, n), out_dtype),
        grid=(t // tm,),
        in_specs=[pl.BlockSpec((tm, d), lambda i: (i, 0)),
                  _resident((1, d), lambda i: (0, 0)),
                  _resident((d, n), lambda i: (0, 0))],
        out_specs=pl.BlockSpec((tm, n), lambda i: (i, 0)),
        scratch_shapes=[pltpu.VMEM((tm, d), BF16)],
        compiler_params=_cparams(("parallel",), 48),
        name="norm_matmul",
    )(x2, g.reshape(1, d), w)


def _inproj_conv_kernel(x_ref, g_ref, w_ref, cw_ref, cb_ref, o_ref, hn_sc, hbuf_sc, halo_sc,
                        *, tm, tn, n_conv, steps_per_batch):
    taps = cw_ref.shape[0]

    @pl.when(pl.program_id(0) % steps_per_batch == 0)
    def _():
        halo_sc[...] = jnp.zeros_like(halo_sc)

    hn_sc[...] = _rms_rows(x_ref[...], g_ref[...]).astype(BF16)
    for j in range(w_ref.shape[1] // tn):
        cols = slice(j * tn, (j + 1) * tn)
        y = _dot(hn_sc[...], w_ref[:, cols])
        if j * tn < n_conv:
            buf = hbuf_sc.at[j % 2]
            buf[0:SUBLANES, :] = halo_sc[j]
            buf[SUBLANES:SUBLANES + tm, :] = y
            halo_sc[j] = buf[tm:tm + SUBLANES, :]
            y = cb_ref[:, cols]
            for k in range(taps):
                off = SUBLANES - (taps - 1) + k
                y = y + cw_ref[k:k + 1, cols] * buf[off:off + tm, :]
            y = _silu(y)
        o_ref[:, cols] = y.astype(o_ref.dtype)


def _inproj_conv(x2, g, w, conv_w, conv_b, seq, tn, tm):
    t, d = x2.shape
    n = w.shape[1]
    n_conv = conv_w.shape[1]
    tm = min(tm, seq)
    kern = functools.partial(_inproj_conv_kernel, tm=tm, tn=tn, n_conv=n_conv, steps_per_batch=seq // tm)
    return pl.pallas_call(
        kern,
        out_shape=jax.ShapeDtypeStruct((t, n), BF16),
        grid=(t // tm,),
        in_specs=[pl.BlockSpec((tm, d), lambda i: (i, 0)),
                  _resident((1, d), lambda i: (0, 0)),
                  _resident((d, n), lambda i: (0, 0)),
                  _resident(conv_w.shape, lambda i: (0, 0)),
                  _resident((1, n_conv), lambda i: (0, 0))],
        out_specs=pl.BlockSpec((tm, n), lambda i: (i, 0)),
        scratch_shapes=[pltpu.VMEM((tm, d), BF16),
                        pltpu.VMEM((2, tm + SUBLANES, tn), F32),
                        pltpu.VMEM((n_conv // tn, SUBLANES, tn), F32)],
        compiler_params=_cparams(("arbitrary",), 48),
        name="inproj_conv",
    )(x2, g.reshape(1, d), w, conv_w, conv_b.reshape(1, -1))


def _matmul_residual_kernel(a_ref, w_ref, r_ref, o_ref):
    o_ref[...] = r_ref[...] + _dot(a_ref[...], w_ref[...])


def _matmul_residual(a, w, res):
    t, k = a.shape
    n = w.shape[1]
    tm = min(512, t)
    return pl.pallas_call(
        _matmul_residual_kernel,
        out_shape=jax.ShapeDtypeStruct((t, n), F32),
        grid=(t // tm,),
        in_specs=[pl.BlockSpec((tm, k), lambda i: (i, 0)),
                  _resident((k, n), lambda i: (0, 0)),
                  pl.BlockSpec((tm, n), lambda i: (i, 0))],
        out_specs=pl.BlockSpec((tm, n), lambda i: (i, 0)),
        compiler_params=_cparams(("parallel",), 40),
        name="matmul_residual",
    )(a, w, res)


def _ffn_kernel(x_ref, g_ref, wup_ref, cw_ref, cb_ref, wdn_ref, o_ref,
                hn_sc, hbuf_sc, halo_sc, act_sc, *, tm, ffn, steps_per_batch):
    cols = FFN_COLS
    nchunk = ffn // cols

    @pl.when(pl.program_id(0) % steps_per_batch == 0)
    def _():
        halo_sc[...] = jnp.zeros_like(halo_sc)

    hn_sc[...] = _rms_rows(x_ref[...], g_ref[...]).astype(BF16)

    def up(c):
        lo = c * cols
        buf = hbuf_sc.at[c % 2]
        hn = hn_sc[...]
        buf[0:SUBLANES, :] = halo_sc[c]
        buf[SUBLANES:SUBLANES + tm, 0:cols] = _dot(hn, wup_ref[:, lo:lo + cols])
        buf[SUBLANES:SUBLANES + tm, cols:2 * cols] = _dot(hn, wup_ref[:, ffn + lo:ffn + lo + cols])
        halo_sc[c] = buf[tm:tm + SUBLANES, :]

    def activate(c):
        lo = c * cols
        buf = hbuf_sc.at[c % 2]
        w3 = jnp.concatenate([cw_ref[:, lo:lo + cols], cw_ref[:, ffn + lo:ffn + lo + cols]], axis=1)
        y = jnp.concatenate([cb_ref[:, lo:lo + cols], cb_ref[:, ffn + lo:ffn + lo + cols]], axis=1)
        for k in range(FFN_CONV):
            off = SUBLANES - (FFN_CONV - 1) + k
            y = y + w3[k:k + 1, :] * buf[off:off + tm, :]
        act_sc[:, lo:lo + cols] = (_silu(y[:, :cols]) * y[:, cols:]).astype(BF16)

    up(0)
    for c in range(nchunk):
        if c + 1 < nchunk:
            up(c + 1)
        activate(c)
    o_ref[...] = x_ref[...] + _dot(act_sc[...], wdn_ref[...])


def _conv_ffn(x2, g, w_up, conv_w, conv_b, w_down, seq):
    t, d = x2.shape
    ffn = w_down.shape[0]
    tm = min(FFN_ROWS, seq)
    kern = functools.partial(_ffn_kernel, tm=tm, ffn=ffn, steps_per_batch=seq // tm)
    return pl.pallas_call(
        kern,
        out_shape=jax.ShapeDtypeStruct((t, d), F32),
        grid=(t // tm,),
        in_specs=[pl.BlockSpec((tm, d), lambda i: (i, 0)),
                  _resident((1, d), lambda i: (0, 0)),
                  _resident((d, 2 * ffn), lambda i: (0, 0)),
                  _resident((FFN_CONV, 2 * ffn), lambda i: (0, 0)),
                  _resident((1, 2 * ffn), lambda i: (0, 0)),
                  _resident((ffn, d), lambda i: (0, 0))],
        out_specs=pl.BlockSpec((tm, d), lambda i: (i, 0)),
        scratch_shapes=[pltpu.VMEM((tm, d), BF16),
                        pltpu.VMEM((2, tm + SUBLANES, 2 * FFN_COLS), F32),
                        pltpu.VMEM((ffn // FFN_COLS, SUBLANES, 2 * FFN_COLS), F32),
                        pltpu.VMEM((tm, ffn), BF16)],
        compiler_params=_cparams(("arbitrary",), 48),
        name="conv_ffn",
    )(x2, g.reshape(1, d), w_up.astype(BF16), conv_w, conv_b.reshape(1, -1), w_down.astype(BF16))


def _qkv_kernel(x_ref, g_ref, w_ref, wr_ref, cos_ref, sin_ref, o_ref, hn_sc, *, tn):
    hn_sc[...] = _rms_rows(x_ref[...], g_ref[...]).astype(BF16)
    reps = tn // LANES
    n_rope = wr_ref.shape[1]
    for j in range(w_ref.shape[1] // tn):
        cols = slice(j * tn, (j + 1) * tn)
        y = _dot(hn_sc[...], w_ref[:, cols])
        if j * tn < n_rope:
            cos = jnp.concatenate([cos_ref[...]] * reps, axis=1)
            sin = jnp.concatenate([sin_ref[...]] * reps, axis=1)
            y = y * cos + _dot(hn_sc[...], wr_ref[:, cols]) * sin
        o_ref[:, cols] = y.astype(o_ref.dtype)


def _rope_tables(seq):
    half = ROPE_DIM // 2
    pos = jnp.arange(seq, dtype=F32)
    inv_freq = ROPE_THETA ** (-jnp.arange(0, ROPE_DIM, 2, dtype=F32) / ROPE_DIM)
    ang = pos[:, None] * inv_freq[None, :]
    cos, sin = jnp.cos(ang), jnp.sin(ang)
    lane = jnp.arange(LANES) % DA_HEAD_DIM
    idx = lane % half
    cos_l = jnp.where(lane[None, :] < ROPE_DIM, cos[:, idx], 1.0)
    sin_l = jnp.where(lane[None, :] < ROPE_DIM, sin[:, idx], 0.0)
    return cos_l, sin_l


def _rope_partner_weight(w):
    half = ROPE_DIM // 2
    col = jnp.arange(w.shape[1])
    lane = col % DA_HEAD_DIM
    partner = jnp.where(lane < half, col + half, col - half)
    sign = jnp.where(lane < half, -1.0, jnp.where(lane < ROPE_DIM, 1.0, 0.0))
    return w[:, jnp.clip(partner, 0, w.shape[1] - 1)] * sign[None, :]


def _qkv_proj(x2, g, w_qkv, seq):
    t, d = x2.shape
    width = w_qkv.shape[1] // 3
    scale = DA_HEAD_DIM ** -0.5 * math.log2(math.e)
    w = jnp.concatenate([w_qkv[:, :width] * scale, w_qkv[:, width:]], axis=1).astype(BF16)
    wr = _rope_partner_weight(w[:, :2 * width])
    n = w.shape[1]
    tm = min(512, seq)
    tn = 512
    cos_l, sin_l = _rope_tables(seq)
    spb = seq // tm
    tab = pl.BlockSpec((tm, LANES), lambda i: (i % spb, 0))
    return pl.pallas_call(
        functools.partial(_qkv_kernel, tn=tn),
        out_shape=jax.ShapeDtypeStruct((t, n), BF16),
        grid=(t // tm,),
        in_specs=[pl.BlockSpec((tm, d), lambda i: (i, 0)),
                  _resident((1, d), lambda i: (0, 0)),
                  _resident((d, n), lambda i: (0, 0)),
                  _resident((d, 2 * width), lambda i: (0, 0)),
                  tab, tab],
        out_specs=pl.BlockSpec((tm, n), lambda i: (i, 0)),
        scratch_shapes=[pltpu.VMEM((tm, d), BF16)],
        compiler_params=_cparams(("parallel",), 40),
        name="qkv_rope",
    )(x2, g.reshape(1, d), w, wr, cos_l, sin_l)


def _attn_kernel(q_ref, k_ref, v_ref, lq1_ref, lk1_ref, lq2_ref, lk2_ref, gain_ref, o_ref,
                 vt_sc, qt_sc, s_sc, mx_sc, acc_sc, m_sc, l_sc, *, tq, seq, lambda_init):
    qi = pl.program_id(2)
    nkv = seq // tq

    @pl.when(qi == 0)
    def _():
        def tr(t, carry):
            st = pl.multiple_of(t * tq, tq)
            vt_sc[t] = v_ref[pl.ds(st, tq), :].astype(F32).T.astype(BF16)
            return carry
        lax.fori_loop(0, nkv, tr, 0)

    qt = q_ref[...].astype(F32).T
    row = lax.broadcasted_iota(jnp.int32, qt.shape, 0)
    qt_sc[0] = jnp.where(row < DA_HEAD_DIM, qt, 0.0).astype(BF16)
    qt_sc[1] = jnp.where(row >= DA_HEAD_DIM, qt, 0.0).astype(BF16)

    m_sc[...] = jnp.full_like(m_sc, -jnp.inf)
    l_sc[...] = jnp.zeros_like(l_sc)
    acc_sc[...] = jnp.zeros_like(acc_sc)

    def scores(t, slot, masked):
        st = pl.multiple_of(t * tq, tq)
        kt = k_ref[pl.ds(st, tq), :]
        if masked:
            kc = _idiv(lax.broadcasted_iota(jnp.int32, (tq, tq), 0), CHUNK)
            qc = _idiv(lax.broadcasted_iota(jnp.int32, (tq, tq), 1), CHUNK)
            keep = kc <= qc
        for c in range(2):
            s = _dot(kt, qt_sc[c])
            if masked:
                s = jnp.where(keep, s, -jnp.inf)
            s_sc[slot, c] = s
            mx_sc[slot, c] = jnp.max(s, axis=0, keepdims=True)

    def accumulate(t, slot):
        vt = vt_sc[t]
        for c in range(2):
            m_prev = m_sc[c]
            m_new = jnp.maximum(m_prev, mx_sc[slot, c])
            alpha = jnp.exp2(m_prev - m_new)
            p = jnp.exp2(s_sc[slot, c] - m_new)
            l_sc[c] = alpha * l_sc[c] + jnp.sum(p, axis=0, keepdims=True)
            acc_sc[c] = alpha * acc_sc[c] + _dot(vt, p.astype(BF16))
            m_sc[c] = m_new

    scores(qi, 0, True)

    def run(width):
        def body(u, carry):
            for j in range(width):
                t = width * u + j
                scores(t, (j + 1) % 2, False)
                accumulate(jnp.where(t == 0, qi, t - 1), j % 2)
            return carry
        return body

    nquad = qi // ATTN_UNROLL
    npair = qi // 2
    lax.fori_loop(0, nquad, run(ATTN_UNROLL), 0)
    lax.fori_loop(nquad * (ATTN_UNROLL // 2), npair, run(2), 0)
    odd = qi % 2 == 1

    @pl.when(odd)
    def _():
        scores(qi - 1, 1, False)
        accumulate(jnp.where(qi == 1, qi, qi - 2), 0)
        accumulate(qi - 1, 1)

    @pl.when(jnp.logical_not(odd))
    def _():
        accumulate(jnp.where(qi == 0, qi, qi - 1), 0)

    lam =(jnp.exp(jnp.sum(lq1_ref[...] * lk1_ref[...], axis=-1, keepdims=True))
           - jnp.exp(jnp.sum(lq2_ref[...] * lk2_ref[...], axis=-1, keepdims=True)) + lambda_init)
    o = acc_sc[0] / l_sc[0] - lam * (acc_sc[1] / l_sc[1])
    ms = jnp.mean(o * o, axis=0, keepdims=True)
    on = (o * lax.rsqrt(ms + EPS)).T
    o_ref[...] = (on * gain_ref[...] * (1.0 - lambda_init)).astype(o_ref.dtype)


def _diff_attention_core(qkv, lq1, lk1, lq2, lk2, sub_gain, lambda_init, batch, seq):
    width = DA_HEADS * DA_VALUE_DIM
    qkv3 = qkv.reshape(batch, seq, 3 * width)
    tq = min(512, seq)
    kern = functools.partial(_attn_kernel, tq=tq, seq=seq, lambda_init=lambda_init)
    vec = pl.BlockSpec((1, DA_HEAD_DIM), lambda b, h, i: (0, 0))
    return pl.pallas_call(
        kern,
        out_shape=jax.ShapeDtypeStruct((batch, seq, width), BF16),
        grid=(batch, DA_HEADS, seq // tq),
        in_specs=[pl.BlockSpec((None, tq, DA_VALUE_DIM), lambda b, h, i: (b, i, h)),
                  pl.BlockSpec((None, seq, DA_VALUE_DIM), lambda b, h, i: (b, 0, DA_HEADS + h)),
                  pl.BlockSpec((None, seq, DA_VALUE_DIM), lambda b, h, i: (b, 0, 2 * DA_HEADS + h)),
                  vec, vec, vec, vec,
                  pl.BlockSpec((1, DA_VALUE_DIM), lambda b, h, i: (0, 0))],
        out_specs=pl.BlockSpec((None, tq, DA_VALUE_DIM), lambda b, h, i: (b, i, h)),
        scratch_shapes=[pltpu.VMEM((seq // tq, DA_VALUE_DIM, tq), BF16),
                        pltpu.VMEM((2, DA_VALUE_DIM, tq), BF16),
                        pltpu.VMEM((2, 2, tq, tq), F32),
                        pltpu.VMEM((2, 2, 1, tq), F32),
                        pltpu.VMEM((2, DA_VALUE_DIM, tq), F32),
                        pltpu.VMEM((2, 1, tq), F32),
                        pltpu.VMEM((2, 1, tq), F32)],
        compiler_params=_cparams(("parallel", "parallel", "arbitrary"), 48),
        name="diff_attention",
    )(qkv3, qkv3, qkv3, lq1.reshape(1, -1), lk1.reshape(1, -1), lq2.reshape(1, -1),
      lk2.reshape(1, -1), sub_gain.reshape(1, -1)).reshape(batch * seq, width)


def _ssd_kernel(xbc_ref, z_ref, dt_ref, dtb_ref, alog_ref, dsk_ref, ng_ref,
                tril_ref, e128_ref, e64_ref, y_ref, state_sc, *, steps_per_batch, inner):
    L = SSD_CHUNK
    G = SSM_GROUPS
    N = SSM_STATE
    gw = inner // G
    R = gw // SSM_HEAD_DIM

    @pl.when(pl.program_id(0) % steps_per_batch == 0)
    def _():
        state_sc[...] = jnp.zeros_like(state_sc)

    xbc = xbc_ref[...].astype(F32)

    x = dt_ref[...] + dtb_ref[...]
    dt = jnp.maximum(x, 0.0) + jnp.log(1.0 + jnp.exp(-jnp.abs(x)))
    adt = dt * (-jnp.exp(alog_ref[...]))

    acum = _dot3_lhs01(tril_ref[...], adt)
    acol128 = _dot3_rhs01(acum, e128_ref[...])
    acol64 = _dot3_rhs01(acum, e64_ref[...])
    dt64 = _dot3_rhs01(dt, e64_ref[...])

    ri = lax.broadcasted_iota(jnp.int32, acol128.shape, 0)
    ci = lax.broadcasted_iota(jnp.int32, acol128.shape, 1) & (L - 1)
    arow = jnp.sum(jnp.where(ri == ci, acol128, 0.0), axis=0, keepdims=True)
    decay_in = jnp.exp(jnp.where(ci <= ri, acol128 - arow, -jnp.inf))

    xs = xbc[:, :inner]
    xdt = xs * dt64
    last = acol64[L - 1:L, :]
    xdo = (xdt * jnp.exp(last - acol64)).astype(BF16)
    xdt = xdt.astype(BF16)
    e_a = jnp.exp(acol64)
    e_last = jnp.exp(last)

    rr = _idiv(lax.broadcasted_iota(jnp.int32, (R * L, gw), 0), L)
    cc = _idiv(lax.broadcasted_iota(jnp.int32, (R * L, gw), 1), SSM_HEAD_DIM)
    blockdiag = rr == cc

    for g in range(G):
        bg = xbc[:, inner + g * N: inner + (g + 1) * N]
        cg = xbc[:, inner + G * N + g * N: inner + G * N + (g + 1) * N].astype(BF16)
        bgt = bg.T.astype(BF16)
        cb = _dot(cg, bgt)
        cbt = jnp.concatenate([cb] * R, axis=1)
        mg = (cbt * decay_in[:, g * R * L:(g + 1) * R * L]).astype(BF16)
        xg = xdt[:, g * gw:(g + 1) * gw]
        xbd = jnp.where(blockdiag, jnp.concatenate([xg] * R, axis=0), jnp.zeros((), BF16))
        y = _dot(mg, xbd)
        st = state_sc[g]
        y = y + _dot(cg, st.astype(BF16)) * e_a[:, g * gw:(g + 1) * gw]
        state_sc[g] = st * e_last[:, g * gw:(g + 1) * gw] + _dot(bgt, xdo[:, g * gw:(g + 1) * gw])
        y = y + dsk_ref[:, g * gw:(g + 1) * gw] * xs[:, g * gw:(g + 1) * gw]
        y = y * _silu(z_ref[:, g * gw:(g + 1) * gw].astype(F32))
        ms = jnp.mean(y * y, axis=-1, keepdims=True)
        y_ref[:, g * gw:(g + 1) * gw] = (y * lax.rsqrt(ms + EPS) * ng_ref[:, g * gw:(g + 1) * gw]).astype(y_ref.dtype)


def _mamba2_mixer(x2, g, w_in, conv_w, conv_b, dt_bias, a_log, d_skip, norm_gain, w_out, seq):
    t, d = x2.shape
    nh = dt_bias.shape[0]
    inner = nh * SSM_HEAD_DIM
    conv_dim = conv_w.shape[1]
    L = SSD_CHUNK
    w_xz = jnp.concatenate([w_in[:, inner:inner + conv_dim], w_in[:, :inner]], axis=1).astype(BF16)
    w_dt = jnp.pad(w_in[:, inner + conv_dim:], ((0, 0), (0, LANES - nh))).astype(BF16)
    xz = _inproj_conv(x2, g, w_xz, conv_w, conv_b, seq, 1024, 512)
    dtr = _norm_matmul(x2, g, w_dt, F32, LANES, 512)

    pad1 = lambda v: jnp.pad(v, (0, LANES - nh)).reshape(1, LANES)
    rows = jnp.arange(L)
    tril = (rows[None, :] <= rows[:, None]).astype(BF16)
    head = jnp.arange(LANES)
    e128 = (head[:, None] == (jnp.arange(nh * L) // L)[None, :]).astype(BF16)
    e64 = (head[:, None] == (jnp.arange(inner) // SSM_HEAD_DIM)[None, :]).astype(BF16)
    dsk = jnp.repeat(d_skip, SSM_HEAD_DIM).reshape(1, inner)

    nblk_x = conv_dim // inner
    kern = functools.partial(_ssd_kernel, steps_per_batch=seq // L, inner=inner)
    const = lambda shape: _resident(shape, lambda i: (0, 0))
    y = pl.pallas_call(
        kern,
        out_shape=jax.ShapeDtypeStruct((t, inner), BF16),
        grid=(t // L,),
        in_specs=[pl.BlockSpec((L, conv_dim), lambda i: (i, 0)),
                  pl.BlockSpec((L, inner), lambda i: (i, nblk_x)),
                  pl.BlockSpec((L, LANES), lambda i: (i, 0)),
                  const((1, LANES)), const((1, LANES)),
                  const((1, inner)), const((1, inner)),
                  const((L, L)), const((LANES, nh * L)), const((LANES, inner))],
        out_specs=pl.BlockSpec((L, inner), lambda i: (i, 0)),
        scratch_shapes=[pltpu.VMEM((SSM_GROUPS, SSM_STATE, inner // SSM_GROUPS), F32)],
        compiler_params=_cparams(("arbitrary",), 48),
        name="ssd_scan",
    )(xz, xz, dtr, pad1(dt_bias), pad1(a_log), dsk,
      norm_gain.reshape(1, -1), tril, e128, e64)
    return _matmul_residual(y, w_out.astype(BF16), x2)


def _s5_tables(a_re, a_im, log_dt, b_re, b_im, c_re, c_im, d_skip):
    L = S5_CHUNK
    G, P = a_re.shape
    K = b_re.shape[-1]
    lam = lax.complex(jnp.minimum(a_re, -1e-4), a_im)
    delta = jnp.exp(log_dt)[:, None]
    lam_bar = jnp.exp(lam * delta)
    b_bar = ((lam_bar - 1.0) / lam)[..., None] * lax.complex(b_re, b_im)
    c = lax.complex(c_re, c_im)
    tau = jnp.arange(L + 1, dtype=F32)
    pw = jnp.exp(tau[:, None, None] * (lam * delta)[None])

    kt = jnp.einsum("gkp,tgp,gpi->gtki", c, pw[:L], b_bar).real
    lag = jnp.arange(L)[None, :] - jnp.arange(L)[:, None]
    kg = jnp.where((lag >= 0)[None, :, :, None, None], kt[:, jnp.maximum(lag, 0)], 0.0)
    m = kg.transpose(0, 1, 4, 2, 3).reshape(G, L * K, L * K)

    bst = pw[:L][::-1].transpose(1, 0, 2)[:, :, None, :] * b_bar.transpose(0, 2, 1)[:, None, :, :]
    bst = bst.reshape(G, L * K, P)
    w = c.transpose(0, 2, 1)[:, :, None, :] * pw[1:].transpose(1, 2, 0)[:, :, :, None]
    w = w.reshape(G, P, L * K)

    par = (jnp.arange(G) % 2)[:, None, None, None] == jnp.arange(2)[None, None, :, None]
    put_cols = lambda v: jnp.where(par, v[:, :, None, :], 0.0).reshape(G, L * K, 2 * P)
    par_r = (jnp.arange(G) % 2)[:, None, None, None] == jnp.arange(2)[None, :, None, None]
    put_rows = lambda v: jnp.where(par_r, v[:, None, :, :], 0.0).reshape(G, 2 * P, L * K)
    lam_l = pw[L].reshape(G // 2, 1, 2 * P)
    dsk = jnp.tile(d_skip[:, None, :], (1, L, 1)).reshape(G, 1, L * K)
    return dict(
        m=m.astype(BF16),
        bst_re=put_cols(bst.real).astype(BF16), bst_im=put_cols(bst.imag).astype(BF16),
        cst_re=put_rows(w.real).astype(BF16), cst_im=put_rows(-w.imag).astype(BF16),
        lam_re=lam_l.real, lam_im=lam_l.imag, dsk=dsk)


def _s5_kernel(u_ref, m_ref, bre_ref, bim_ref, cre_ref, cim_ref, lre_ref, lim_ref, dsk_ref, o_ref,
               s_re_sc, s_im_sc, in_re_sc, in_im_sc, *, batch, rows_per_batch):
    s_re_sc[...] = _dot(u_ref[0], bre_ref[0]) + _dot(u_ref[1], bre_ref[1])
    s_im_sc[...] = _dot(u_ref[0], bim_ref[0]) + _dot(u_ref[1], bim_ref[1])
    lre = lre_ref[0]
    lim = lim_ref[0]

    def body(c, carry):
        out = []
        for b in range(batch):
            sre, sim = carry[b]
            r = b * rows_per_batch + c
            in_re_sc[pl.ds(r, 1), :] = sre
            in_im_sc[pl.ds(r, 1), :] = sim
            out.append((lre * sre - lim * sim + s_re_sc[pl.ds(r, 1), :],
                        lre * sim + lim * sre + s_im_sc[pl.ds(r, 1), :]))
        return tuple(out)

    zero = jnp.zeros((1, lre.shape[-1]), F32)
    lax.fori_loop(0, rows_per_batch, body, tuple((zero, zero) for _ in range(batch)))

    sre = in_re_sc[...].astype(BF16)
    sim = in_im_sc[...].astype(BF16)
    for gi in range(2):
        u = u_ref[gi]
        y = _dot(u, m_ref[gi]) + _dot(sre, cre_ref[gi]) + _dot(sim, cim_ref[gi])
        y = y + dsk_ref[gi] * u.astype(F32)
        inner = math.sqrt(2.0 / math.pi) * (y + 0.044715 * (y * y * y))
        o_ref[gi] = (0.5 * y * (1.0 + jnp.tanh(inner))).astype(o_ref.dtype)


def _glu_residual_kernel(a_ref, w_ref, r_ref, o_ref, *, n):
    a = a_ref[...]
    val = _dot(a, w_ref[:, :n])
    gate = _dot(a, w_ref[:, n:])
    o_ref[...] = r_ref[...] + val * _sigmoid(gate)


def _s5_mixer(x2, g, a_re, a_im, log_dt, b_re, b_im, c_re, c_im, d_skip, w_glu, batch, seq):
    t, d = x2.shape
    L = S5_CHUNK
    G, K = d_skip.shape
    tab = _s5_tables(a_re, a_im, log_dt, b_re, b_im, c_re, c_im, d_skip)
    u = _rmsnorm(x2, g, BF16)
    rows = t // L
    lk = L * K
    ug = u.reshape(rows, L, G, K).transpose(2, 0, 1, 3).reshape(G, rows, lk)
    kern = functools.partial(_s5_kernel, batch=batch, rows_per_batch=seq // L)
    pair = lambda shape: pl.BlockSpec(shape, lambda i: (i, 0, 0))
    p2 = 2 * S5_STATE
    ya = pl.pallas_call(
        kern,
        out_shape=jax.ShapeDtypeStruct((G, rows, lk), BF16),
        grid=(G // 2,),
        in_specs=[pair((2, rows, lk)), pair((2, lk, lk)),
                  pair((2, lk, p2)), pair((2, lk, p2)),
                  pair((2, p2, lk)), pair((2, p2, lk)),
                  pair((1, 1, p2)), pair((1, 1, p2)), pair((2, 1, lk))],
        out_specs=pair((2, rows, lk)),
        scratch_shapes=[pltpu.VMEM((rows, p2), F32)] * 4,
        compiler_params=_cparams(("parallel",), 40),
        name="s5_scan",
    )(ug, tab["m"], tab["bst_re"], tab["bst_im"], tab["cst_re"], tab["cst_im"],
      tab["lam_re"], tab["lam_im"], tab["dsk"])
    ya = ya.reshape(G, rows, L, K).transpose(1, 2, 0, 3).reshape(t, d)
    tm = min(512, t)
    return pl.pallas_call(
        functools.partial(_glu_residual_kernel, n=d),
        out_shape=jax.ShapeDtypeStruct((t, d), F32),
        grid=(t // tm,),
        in_specs=[pl.BlockSpec((tm, d), lambda i: (i, 0)),
                  _resident((d, 2 * d), lambda i: (0, 0)),
                  pl.BlockSpec((tm, d), lambda i: (i, 0))],
        out_specs=pl.BlockSpec((tm, d), lambda i: (i, 0)),
        compiler_params=_cparams(("parallel",), 40),
        name="glu_residual",
    )(ya, w_glu.astype(BF16), x2)


def kernel(x, norm_mix, norm_ffn, norm_final, attn_w_qkv, attn_w_o, attn_lambda_q1, attn_lambda_k1, attn_lambda_q2, attn_lambda_k2, attn_sub_gain, ssm_w_in, ssm_conv_w, ssm_conv_b, ssm_dt_bias, ssm_a_log, ssm_d, ssm_norm_gain, ssm_w_out, s5_a_re, s5_a_im, s5_log_dt, s5_b_re, s5_b_im, s5_c_re, s5_c_im, s5_d, s5_w_glu, ffn_w_up, ffn_conv_w, ffn_conv_b, ffn_w_down):
    batch, seq, d = x.shape
    depth = norm_mix.shape[0]
    x2 = x.reshape(batch * seq, d)
    for i in range(depth):
        kind = i % N_MIXERS
        j = i // N_MIXERS
        if kind == 0:
            lambda_init = 0.8 - 0.6 * math.exp(-0.3 * i)
            qkv = _qkv_proj(x2, norm_mix[i], attn_w_qkv[j], seq)
            o = _diff_attention_core(qkv, attn_lambda_q1[j], attn_lambda_k1[j], attn_lambda_q2[j],
                                     attn_lambda_k2[j], attn_sub_gain[j], lambda_init, batch, seq)
            x2 = _matmul_residual(o, attn_w_o[j].astype(BF16), x2)
        elif kind == 1:
            x2 = _mamba2_mixer(x2, norm_mix[i], ssm_w_in[j], ssm_conv_w[j], ssm_conv_b[j], ssm_dt_bias[j],
                               ssm_a_log[j], ssm_d[j], ssm_norm_gain[j], ssm_w_out[j], seq)
        else:
            x2 = _s5_mixer(x2, norm_mix[i], s5_a_re[j], s5_a_im[j], s5_log_dt[j], s5_b_re[j], s5_b_im[j],
                           s5_c_re[j], s5_c_im[j], s5_d[j], s5_w_glu[j], batch, seq)
        x2 = _conv_ffn(x2, norm_ffn[i], ffn_w_up[i], ffn_conv_w[i], ffn_conv_b[i], ffn_w_down[i], seq)
    return _rmsnorm(x2, norm_final, F32).reshape(batch, seq, d)
```

```python
import functools
import math

import jax
import jax.numpy as jnp
from jax import lax
from jax.experimental import pallas as pl
from jax.experimental.pallas import tpu as pltpu

F32 = jnp.float32
BF16 = jnp.bfloat16

EPS = 1e-6
N_MIXERS = 3
CHUNK = 64
DA_HEADS = 8
DA_HEAD_DIM = 64
DA_VALUE_DIM = 2 * DA_HEAD_DIM
ROPE_THETA = 500000.0
ROPE_DIM = DA_HEAD_DIM // 4
SSM_HEAD_DIM = 64
SSM_GROUPS = 8
SSM_STATE = 128
SSM_CONV = 4
SSD_CHUNK = 128
S5_GROUP = 16
S5_STATE = 64
S5_CHUNK = 8
S5_STEPS = 4096
ATTN_UNROLL = 4
FFN_CONV = 3
FFN_COLS = 256
FFN_ROWS = 512
LANES = 128
SUBLANES = 8
MIB = 1024 * 1024


def _cparams(sem, vmem_mib):
    return pltpu.CompilerParams(dimension_semantics=sem, vmem_limit_bytes=vmem_mib * MIB)


def _resident(shape, index_map):
    return pl.BlockSpec(shape, index_map, pipeline_mode=pl.Buffered(1))


def _rms_rows(x, g):
    ms = jnp.mean(x * x, axis=-1, keepdims=True)
    return x * lax.rsqrt(ms + EPS) * g


def _sigmoid(x):
    return 0.5 + 0.5 * jnp.tanh(0.5 * x)


def _silu(x):
    h = 0.5 * x
    return h + h * jnp.tanh(h)


def _idiv(v, n):
    return lax.shift_right_logical(v, int(math.log2(n)))


def _split3(x):
    hi = x.astype(BF16)
    r = x - hi.astype(F32)
    mid = r.astype(BF16)
    lo = (r - mid.astype(F32)).astype(BF16)
    return hi, mid, lo


def _dot(a, b):
    return jnp.dot(a, b, preferred_element_type=F32)


def _dot3_rhs01(x, e):
    hi, mid, lo = _split3(x)
    return _dot(hi, e) + _dot(mid, e) + _dot(lo, e)


def _dot3_lhs01(e, x):
    hi, mid, lo = _split3(x)
    return _dot(e, hi) + _dot(e, mid) + _dot(e, lo)


def _norm_kernel(x_ref, g_ref, o_ref):
    o_ref[...] = _rms_rows(x_ref[...], g_ref[...]).astype(o_ref.dtype)


def _rmsnorm(x2, g, out_dtype):
    t, d = x2.shape
    tm = min(512, t)
    return pl.pallas_call(
        _norm_kernel,
        out_shape=jax.ShapeDtypeStruct((t, d), out_dtype),
        grid=(t // tm,),
        in_specs=[pl.BlockSpec((tm, d), lambda i: (i, 0)),
                  pl.BlockSpec((1, d), lambda i: (0, 0))],
        out_specs=pl.BlockSpec((tm, d), lambda i: (i, 0)),
        compiler_params=_cparams(("parallel",), 32),
        name="rmsnorm",
    )(x2, g.reshape(1, d))


def _norm_matmul_kernel(x_ref, g_ref, w_ref, o_ref, hn_sc, *, tn):
    hn_sc[...] = _rms_rows(x_ref[...], g_ref[...]).astype(BF16)
    for j in range(w_ref.shape[1] // tn):
        o_ref[:, j * tn:(j + 1) * tn] = _dot(hn_sc[...], w_ref[:, j * tn:(j + 1) * tn]).astype(o_ref.dtype)


def _norm_matmul(x2, g, w, out_dtype, tn, tm):
    t, d = x2.shape
    n = w.shape[1]
    tm = min(tm, t)
    return pl.pallas_call(
        functools.partial(_norm_matmul_kernel, tn=tn),
        out_shape=jax.ShapeDtypeStruct((t, n), out_dtype),
        grid=(t // tm,),
        in_specs=[pl.BlockSpec((tm, d), lambda i: (i, 0)),
                  _resident((1, d), lambda i: (0, 0)),
                  _resident((d, n), lambda i: (0, 0))],
        out_specs=pl.BlockSpec((tm, n), lambda i: (i, 0)),
        scratch_shapes=[pltpu.VMEM((tm, d), BF16)],
        compiler_params=_cparams(("parallel",), 48),
        name="norm_matmul",
    )(x2, g.reshape(1, d), w)


def _inproj_conv_kernel(x_ref, g_ref, w_ref, cw_ref, cb_ref, o_ref, hn_sc, hbuf_sc, halo_sc,
                        *, tm, tn, n_conv, steps_per_batch):
    taps = cw_ref.shape[0]

    @pl.when(pl.program_id(0) % steps_per_batch == 0)
    def _():
        halo_sc[...] = jnp.zeros_like(halo_sc)

    hn_sc[...] = _rms_rows(x_ref[...], g_ref[...]).astype(BF16)
    for j in range(w_ref.shape[1] // tn):
        cols = slice(j * tn, (j + 1) * tn)
        y = _dot(hn_sc[...], w_ref[:, cols])
        if j * tn < n_conv:
            buf = hbuf_sc.at[j % 2]
            buf[0:SUBLANES, :] = halo_sc[j]
            buf[SUBLANES:SUBLANES + tm, :] = y
            halo_sc[j] = buf[tm:tm + SUBLANES, :]
            y = cb_ref[:, cols]
            for k in range(taps):
                off = SUBLANES - (taps - 1) + k
                y = y + cw_ref[k:k + 1, cols] * buf[off:off + tm, :]
            y = _silu(y)
        o_ref[:, cols] = y.astype(o_ref.dtype)


def _inproj_conv(x2, g, w, conv_w, conv_b, seq, tn, tm):
    t, d = x2.shape
    n = w.shape[1]
    n_conv = conv_w.shape[1]
    tm = min(tm, seq)
    kern = functools.partial(_inproj_conv_kernel, tm=tm, tn=tn, n_conv=n_conv, steps_per_batch=seq // tm)
    return pl.pallas_call(
        kern,
        out_shape=jax.ShapeDtypeStruct((t, n), BF16),
        grid=(t // tm,),
        in_specs=[pl.BlockSpec((tm, d), lambda i: (i, 0)),
                  _resident((1, d), lambda i: (0, 0)),
                  _resident((d, n), lambda i: (0, 0)),
                  _resident(conv_w.shape, lambda i: (0, 0)),
                  _resident((1, n_conv), lambda i: (0, 0))],
        out_specs=pl.BlockSpec((tm, n), lambda i: (i, 0)),
        scratch_shapes=[pltpu.VMEM((tm, d), BF16),
                        pltpu.VMEM((2, tm + SUBLANES, tn), F32),
                        pltpu.VMEM((n_conv // tn, SUBLANES, tn), F32)],
        compiler_params=_cparams(("arbitrary",), 48),
        name="inproj_conv",
    )(x2, g.reshape(1, d), w, conv_w, conv_b.reshape(1, -1))


def _matmul_residual_kernel(a_ref, w_ref, r_ref, o_ref):
    o_ref[...] = r_ref[...] + _dot(a_ref[...], w_ref[...])


def _matmul_residual(a, w, res):
    t, k = a.shape
    n = w.shape[1]
    tm = min(512, t)
    return pl.pallas_call(
        _matmul_residual_kernel,
        out_shape=jax.ShapeDtypeStruct((t, n), F32),
        grid=(t // tm,),
        in_specs=[pl.BlockSpec((tm, k), lambda i: (i, 0)),
                  _resident((k, n), lambda i: (0, 0)),
                  pl.BlockSpec((tm, n), lambda i: (i, 0))],
        out_specs=pl.BlockSpec((tm, n), lambda i: (i, 0)),
        compiler_params=_cparams(("parallel",), 40),
        name="matmul_residual",
    )(a, w, res)


def _ffn_kernel(x_ref, g_ref, wup_ref, cw_ref, cb_ref, wdn_ref, o_ref,
                hn_sc, hbuf_sc, halo_sc, act_sc, *, tm, ffn, steps_per_batch):
    cols = FFN_COLS
    nchunk = ffn // cols

    @pl.when(pl.program_id(0) % steps_per_batch == 0)
    def _():
        halo_sc[...] = jnp.zeros_like(halo_sc)

    hn_sc[...] = _rms_rows(x_ref[...], g_ref[...]).astype(BF16)

    def up(c):
        lo = c * cols
        buf = hbuf_sc.at[c % 2]
        hn = hn_sc[...]
        buf[0:SUBLANES, :] = halo_sc[c]
        buf[SUBLANES:SUBLANES + tm, 0:cols] = _dot(hn, wup_ref[:, lo:lo + cols])
        buf[SUBLANES:SUBLANES + tm, cols:2 * cols] = _dot(hn, wup_ref[:, ffn + lo:ffn + lo + cols])
        halo_sc[c] = buf[tm:tm + SUBLANES, :]

    def activate(c):
        lo = c * cols
        buf = hbuf_sc.at[c % 2]
        w3 = jnp.concatenate([cw_ref[:, lo:lo + cols], cw_ref[:, ffn + lo:ffn + lo + cols]], axis=1)
        y = jnp.concatenate([cb_ref[:, lo:lo + cols], cb_ref[:, ffn + lo:ffn + lo + cols]], axis=1)
        for k in range(FFN_CONV):
            off = SUBLANES - (FFN_CONV - 1) + k
            y = y + w3[k:k + 1, :] * buf[off:off + tm, :]
        act_sc[:, lo:lo + cols] = (_silu(y[:, :cols]) * y[:, cols:]).astype(BF16)

    up(0)
    for c in range(nchunk):
        if c + 1 < nchunk:
            up(c + 1)
        activate(c)
    o_ref[...] = x_ref[...] + _dot(act_sc[...], wdn_ref[...])


def _conv_ffn(x2, g, w_up, conv_w, conv_b, w_down, seq):
    t, d = x2.shape
    ffn = w_down.shape[0]
    tm = min(FFN_ROWS, seq)
    kern = functools.partial(_ffn_kernel, tm=tm, ffn=ffn, steps_per_batch=seq // tm)
    return pl.pallas_call(
        kern,
        out_shape=jax.ShapeDtypeStruct((t, d), F32),
        grid=(t // tm,),
        in_specs=[pl.BlockSpec((tm, d), lambda i: (i, 0)),
                  _resident((1, d), lambda i: (0, 0)),
                  _resident((d, 2 * ffn), lambda i: (0, 0)),
                  _resident((FFN_CONV, 2 * ffn), lambda i: (0, 0)),
                  _resident((1, 2 * ffn), lambda i: (0, 0)),
                  _resident((ffn, d), lambda i: (0, 0))],
        out_specs=pl.BlockSpec((tm, d), lambda i: (i, 0)),
        scratch_shapes=[pltpu.VMEM((tm, d), BF16),
                        pltpu.VMEM((2, tm + SUBLANES, 2 * FFN_COLS), F32),
                        pltpu.VMEM((ffn // FFN_COLS, SUBLANES, 2 * FFN_COLS), F32),
                        pltpu.VMEM((tm, ffn), BF16)],
        compiler_params=_cparams(("arbitrary",), 48),
        name="conv_ffn",
    )(x2, g.reshape(1, d), w_up.astype(BF16), conv_w, conv_b.reshape(1, -1), w_down.astype(BF16))


def _qkv_kernel(x_ref, g_ref, w_ref, wr_ref, cos_ref, sin_ref, o_ref, hn_sc, *, tn):
    hn_sc[...] = _rms_rows(x_ref[...], g_ref[...]).astype(BF16)
    reps = tn // LANES
    n_rope = wr_ref.shape[1]
    for j in range(w_ref.shape[1] // tn):
        cols = slice(j * tn, (j + 1) * tn)
        y = _dot(hn_sc[...], w_ref[:, cols])
        if j * tn < n_rope:
            cos = jnp.concatenate([cos_ref[...]] * reps, axis=1)
            sin = jnp.concatenate([sin_ref[...]] * reps, axis=1)
            y = y * cos + _dot(hn_sc[...], wr_ref[:, cols]) * sin
        o_ref[:, cols] = y.astype(o_ref.dtype)


def _rope_tables(seq):
    half = ROPE_DIM // 2
    pos = jnp.arange(seq, dtype=F32)
    inv_freq = ROPE_THETA ** (-jnp.arange(0, ROPE_DIM, 2, dtype=F32) / ROPE_DIM)
    ang = pos[:, None] * inv_freq[None, :]
    cos, sin = jnp.cos(ang), jnp.sin(ang)
    lane = jnp.arange(LANES) % DA_HEAD_DIM
    idx = lane % half
    cos_l = jnp.where(lane[None, :] < ROPE_DIM, cos[:, idx], 1.0)
    sin_l = jnp.where(lane[None, :] < ROPE_DIM, sin[:, idx], 0.0)
    return cos_l, sin_l


def _rope_partner_weight(w):
    half = ROPE_DIM // 2
    col = jnp.arange(w.shape[1])
    lane = col % DA_HEAD_DIM
    partner = jnp.where(lane < half, col + half, col - half)
    sign = jnp.where(lane < half, -1.0, jnp.where(lane < ROPE_DIM, 1.0, 0.0))
    return w[:, jnp.clip(partner, 0, w.shape[1] - 1)] * sign[None, :]


def _qkv_proj(x2, g, w_qkv, seq):
    t, d = x2.shape
    width = w_qkv.shape[1] // 3
    scale = DA_HEAD_DIM ** -0.5 * math.log2(math.e)
    w = jnp.concatenate([w_qkv[:, :width] * scale, w_qkv[:, width:]], axis=1).astype(BF16)
    wr = _rope_partner_weight(w[:, :2 * width])
    n = w.shape[1]
    tm = min(512, seq)
    tn = 512
    cos_l, sin_l = _rope_tables(seq)
    spb = seq // tm
    tab = pl.BlockSpec((tm, LANES), lambda i: (i % spb, 0))
    return pl.pallas_call(
        functools.partial(_qkv_kernel, tn=tn),
        out_shape=jax.ShapeDtypeStruct((t, n), BF16),
        grid=(t // tm,),
        in_specs=[pl.BlockSpec((tm, d), lambda i: (i, 0)),
                  _resident((1, d), lambda i: (0, 0)),
                  _resident((d, n), lambda i: (0, 0)),
                  _resident((d, 2 * width), lambda i: (0, 0)),
                  tab, tab],
        out_specs=pl.BlockSpec((tm, n), lambda i: (i, 0)),
        scratch_shapes=[pltpu.VMEM((tm, d), BF16)],
        compiler_params=_cparams(("parallel",), 40),
        name="qkv_rope",
    )(x2, g.reshape(1, d), w, wr, cos_l, sin_l)


def _attn_kernel(q_ref, k_ref, v_ref, lq1_ref, lk1_ref, lq2_ref, lk2_ref, gain_ref, o_ref,
                 vt_sc, qt_sc, s_sc, mx_sc, acc_sc, m_sc, l_sc, *, tq, seq, lambda_init):
    qi = pl.program_id(2)
    nkv = seq // tq

    @pl.when(qi == 0)
    def _():
        def tr(t, carry):
            st = pl.multiple_of(t * tq, tq)
            vt_sc[t] = v_ref[pl.ds(st, tq), :].astype(F32).T.astype(BF16)
            return carry
        lax.fori_loop(0, nkv, tr, 0)

    qt = q_ref[...].astype(F32).T
    row = lax.broadcasted_iota(jnp.int32, qt.shape, 0)
    qt_sc[0] = jnp.where(row < DA_HEAD_DIM, qt, 0.0).astype(BF16)
    qt_sc[1] = jnp.where(row >= DA_HEAD_DIM, qt, 0.0).astype(BF16)

    m_sc[...] = jnp.full_like(m_sc, -jnp.inf)
    l_sc[...] = jnp.zeros_like(l_sc)
    acc_sc[...] = jnp.zeros_like(acc_sc)

    def scores(t, slot, masked):
        st = pl.multiple_of(t * tq, tq)
        kt = k_ref[pl.ds(st, tq), :]
        if masked:
            kc = _idiv(lax.broadcasted_iota(jnp.int32, (tq, tq), 0), CHUNK)
            qc = _idiv(lax.broadcasted_iota(jnp.int32, (tq, tq), 1), CHUNK)
            keep = kc <= qc
        for c in range(2):
            s = _dot(kt, qt_sc[c])
            if masked:
                s = jnp.where(keep, s, -jnp.inf)
            s_sc[slot, c] = s
            mx_sc[slot, c] = jnp.max(s, axis=0, keepdims=True)

    def accumulate(t, slot):
        vt = vt_sc[t]
        for c in range(2):
            m_prev = m_sc[c]
            m_new = jnp.maximum(m_prev, mx_sc[slot, c])
            alpha = jnp.exp2(m_prev - m_new)
            p = jnp.exp2(s_sc[slot, c] - m_new)
            l_sc[c] = alpha * l_sc[c] + jnp.sum(p, axis=0, keepdims=True)
            acc_sc[c] = alpha * acc_sc[c] + _dot(vt, p.astype(BF16))
            m_sc[c] = m_new

    scores(qi, 0, True)

    def run(width):
        def body(u, carry):
            for j in range(width):
                t = width * u + j
                scores(t, (j + 1) % 2, False)
                accumulate(jnp.where(t == 0, qi, t - 1), j % 2)
            return carry
        return body

    nquad = qi // ATTN_UNROLL
    npair = qi // 2
    lax.fori_loop(0, nquad, run(ATTN_UNROLL), 0)
    lax.fori_loop(nquad * (ATTN_UNROLL // 2), npair, run(2), 0)
    odd = qi % 2 == 1

    @pl.when(odd)
    def _():
        scores(qi - 1, 1, False)
        accumulate(jnp.where(qi == 1, qi, qi - 2), 0)
        accumulate(qi - 1, 1)

    @pl.when(jnp.logical_not(odd))
    def _():
        accumulate(jnp.where(qi == 0, qi, qi - 1), 0)

    lam =(jnp.exp(jnp.sum(lq1_ref[...] * lk1_ref[...], axis=-1, keepdims=True))
           - jnp.exp(jnp.sum(lq2_ref[...] * lk2_ref[...], axis=-1, keepdims=True)) + lambda_init)
    o = acc_sc[0] / l_sc[0] - lam * (acc_sc[1] / l_sc[1])
    ms = jnp.mean(o * o, axis=0, keepdims=True)
    on = (o * lax.rsqrt(ms + EPS)).T
    o_ref[...] = (on * gain_ref[...] * (1.0 - lambda_init)).astype(o_ref.dtype)


def _diff_attention_core(qkv, lq1, lk1, lq2, lk2, sub_gain, lambda_init, batch, seq):
    width = DA_HEADS * DA_VALUE_DIM
    qkv3 = qkv.reshape(batch, seq, 3 * width)
    tq = min(512, seq)
    kern = functools.partial(_attn_kernel, tq=tq, seq=seq, lambda_init=lambda_init)
    vec = pl.BlockSpec((1, DA_HEAD_DIM), lambda b, h, i: (0, 0))
    return pl.pallas_call(
        kern,
        out_shape=jax.ShapeDtypeStruct((batch, seq, width), BF16),
        grid=(batch, DA_HEADS, seq // tq),
        in_specs=[pl.BlockSpec((None, tq, DA_VALUE_DIM), lambda b, h, i: (b, i, h)),
                  pl.BlockSpec((None, seq, DA_VALUE_DIM), lambda b, h, i: (b, 0, DA_HEADS + h)),
                  pl.BlockSpec((None, seq, DA_VALUE_DIM), lambda b, h, i: (b, 0, 2 * DA_HEADS + h)),
                  vec, vec, vec, vec,
                  pl.BlockSpec((1, DA_VALUE_DIM), lambda b, h, i: (0, 0))],
        out_specs=pl.BlockSpec((None, tq, DA_VALUE_DIM), lambda b, h, i: (b, i, h)),
        scratch_shapes=[pltpu.VMEM((seq // tq, DA_VALUE_DIM, tq), BF16),
                        pltpu.VMEM((2, DA_VALUE_DIM, tq), BF16),
                        pltpu.VMEM((2, 2, tq, tq), F32),
                        pltpu.VMEM((2, 2, 1, tq), F32),
                        pltpu.VMEM((2, DA_VALUE_DIM, tq), F32),
                        pltpu.VMEM((2, 1, tq), F32),
                        pltpu.VMEM((2, 1, tq), F32)],
        compiler_params=_cparams(("parallel", "parallel", "arbitrary"), 48),
        name="diff_attention",
    )(qkv3, qkv3, qkv3, lq1.reshape(1, -1), lk1.reshape(1, -1), lq2.reshape(1, -1),
      lk2.reshape(1, -1), sub_gain.reshape(1, -1)).reshape(batch * seq, width)


def _ssd_kernel(xbc_ref, z_ref, dt_ref, dtb_ref, alog_ref, dsk_ref, ng_ref,
                tril_ref, e128_ref, e64_ref, y_ref, state_sc, *, steps_per_batch, inner):
    L = SSD_CHUNK
    G = SSM_GROUPS
    N = SSM_STATE
    gw = inner // G
    R = gw // SSM_HEAD_DIM

    @pl.when(pl.program_id(0) % steps_per_batch == 0)
    def _():
        state_sc[...] = jnp.zeros_like(state_sc)

    xbc = xbc_ref[...].astype(F32)

    x = dt_ref[...] + dtb_ref[...]
    dt = jnp.maximum(x, 0.0) + jnp.log(1.0 + jnp.exp(-jnp.abs(x)))
    adt = dt * (-jnp.exp(alog_ref[...]))

    acum = _dot3_lhs01(tril_ref[...], adt)
    acol128 = _dot3_rhs01(acum, e128_ref[...])
    acol64 = _dot3_rhs01(acum, e64_ref[...])
    dt64 = _dot3_rhs01(dt, e64_ref[...])

    ri = lax.broadcasted_iota(jnp.int32, acol128.shape, 0)
    ci = lax.broadcasted_iota(jnp.int32, acol128.shape, 1) & (L - 1)
    arow = jnp.sum(jnp.where(ri == ci, acol128, 0.0), axis=0, keepdims=True)
    decay_in = jnp.exp(jnp.where(ci <= ri, acol128 - arow, -jnp.inf))

    xs = xbc[:, :inner]
    xdt = xs * dt64
    last = acol64[L - 1:L, :]
    xdo = (xdt * jnp.exp(last - acol64)).astype(BF16)
    xdt = xdt.astype(BF16)
    e_a = jnp.exp(acol64)
    e_last = jnp.exp(last)

    rr = _idiv(lax.broadcasted_iota(jnp.int32, (R * L, gw), 0), L)
    cc = _idiv(lax.broadcasted_iota(jnp.int32, (R * L, gw), 1), SSM_HEAD_DIM)
    blockdiag = rr == cc

    for g in range(G):
        bg = xbc[:, inner + g * N: inner + (g + 1) * N]
        cg = xbc[:, inner + G * N + g * N: inner + G * N + (g + 1) * N].astype(BF16)
        bgt = bg.T.astype(BF16)
        cb = _dot(cg, bgt)
        cbt = jnp.concatenate([cb] * R, axis=1)
        mg = (cbt * decay_in[:, g * R * L:(g + 1) * R * L]).astype(BF16)
        xg = xdt[:, g * gw:(g + 1) * gw]
        xbd = jnp.where(blockdiag, jnp.concatenate([xg] * R, axis=0), jnp.zeros((), BF16))
        y = _dot(mg, xbd)
        st = state_sc[g]
        y = y + _dot(cg, st.astype(BF16)) * e_a[:, g * gw:(g + 1) * gw]
        state_sc[g] = st * e_last[:, g * gw:(g + 1) * gw] + _dot(bgt, xdo[:, g * gw:(g + 1) * gw])
        y = y + dsk_ref[:, g * gw:(g + 1) * gw] * xs[:, g * gw:(g + 1) * gw]
        y = y * _silu(z_ref[:, g * gw:(g + 1) * gw].astype(F32))
        ms = jnp.mean(y * y, axis=-1, keepdims=True)
        y_ref[:, g * gw:(g + 1) * gw] = (y * lax.rsqrt(ms + EPS) * ng_ref[:, g * gw:(g + 1) * gw]).astype(y_ref.dtype)


def _mamba2_mixer(x2, g, w_in, conv_w, conv_b, dt_bias, a_log, d_skip, norm_gain, w_out, seq):
    t, d = x2.shape
    nh = dt_bias.shape[0]
    inner = nh * SSM_HEAD_DIM
    conv_dim = conv_w.shape[1]
    L = SSD_CHUNK
    w_xz = jnp.concatenate([w_in[:, inner:inner + conv_dim], w_in[:, :inner]], axis=1).astype(BF16)
    w_dt = jnp.pad(w_in[:, inner + conv_dim:], ((0, 0), (0, LANES - nh))).astype(BF16)
    xz = _inproj_conv(x2, g, w_xz, conv_w, conv_b, seq, 1024, 512)
    dtr = _norm_matmul(x2, g, w_dt, F32, LANES, 512)

    pad1 = lambda v: jnp.pad(v, (0, LANES - nh)).reshape(1, LANES)
    rows = jnp.arange(L)
    tril = (rows[None, :] <= rows[:, None]).astype(BF16)
    head = jnp.arange(LANES)
    e128 = (head[:, None] == (jnp.arange(nh * L) // L)[None, :]).astype(BF16)
    e64 = (head[:, None] == (jnp.arange(inner) // SSM_HEAD_DIM)[None, :]).astype(BF16)
    dsk = jnp.repeat(d_skip, SSM_HEAD_DIM).reshape(1, inner)

    nblk_x = conv_dim // inner
    kern = functools.partial(_ssd_kernel, steps_per_batch=seq // L, inner=inner)
    const = lambda shape: _resident(shape, lambda i: (0, 0))
    y = pl.pallas_call(
        kern,
        out_shape=jax.ShapeDtypeStruct((t, inner), BF16),
        grid=(t // L,),
        in_specs=[pl.BlockSpec((L, conv_dim), lambda i: (i, 0)),
                  pl.BlockSpec((L, inner), lambda i: (i, nblk_x)),
                  pl.BlockSpec((L, LANES), lambda i: (i, 0)),
                  const((1, LANES)), const((1, LANES)),
                  const((1, inner)), const((1, inner)),
                  const((L, L)), const((LANES, nh * L)), const((LANES, inner))],
        out_specs=pl.BlockSpec((L, inner), lambda i: (i, 0)),
        scratch_shapes=[pltpu.VMEM((SSM_GROUPS, SSM_STATE, inner // SSM_GROUPS), F32)],
        compiler_params=_cparams(("arbitrary",), 48),
        name="ssd_scan",
    )(xz, xz, dtr, pad1(dt_bias), pad1(a_log), dsk,
      norm_gain.reshape(1, -1), tril, e128, e64)
    return _matmul_residual(y, w_out.astype(BF16), x2)


def _s5_tables(a_re, a_im, log_dt, b_re, b_im, c_re, c_im, d_skip):
    L = S5_CHUNK
    G, P = a_re.shape
    K = b_re.shape[-1]
    lam = lax.complex(jnp.minimum(a_re, -1e-4), a_im)
    delta = jnp.exp(log_dt)[:, None]
    lam_bar = jnp.exp(lam * delta)
    b_bar = ((lam_bar - 1.0) / lam)[..., None] * lax.complex(b_re, b_im)
    c = lax.complex(c_re, c_im)
    tau = jnp.arange(L + 1, dtype=F32)
    pw = jnp.exp(tau[:, None, None] * (lam * delta)[None])

    gs = LANES // K
    ns = G // gs
    eye = jnp.eye(gs, dtype=F32)
    kt = jnp.einsum("gkp,tgp,gpi->gtki", c, pw[:L], b_bar).real
    lag = jnp.arange(L)[None, :] - jnp.arange(L)[:, None]
    kg = jnp.where((lag >= 0)[None, :, :, None, None], kt[:, jnp.maximum(lag, 0)], 0.0)
    kg = kg.reshape(ns, gs, L, L, K, K)
    m = jnp.einsum("sgjtoi,gh->sjgitho", kg, eye).reshape(ns, L * LANES, L * LANES)

    bst = pw[:L][::-1].transpose(1, 0, 2)[:, :, None, :] * b_bar.transpose(0, 2, 1)[:, None, :, :]
    bst = bst.reshape(ns, gs, L, K, P)
    put_b = lambda v: jnp.einsum("sgjip,gh->sjgihp", v, eye).reshape(ns, L * LANES, gs * P)
    w = c.transpose(0, 2, 1)[:, :, None, :] * pw[1:].transpose(1, 2, 0)[:, :, :, None]
    w = w.reshape(ns, gs, P, L, K)
    put_c = lambda v: jnp.einsum("sgpto,gh->sgptho", v, eye).reshape(ns, gs * P, L * LANES)
    lam_l = pw[L].reshape(ns, 1, gs * P)
    return dict(
        m=m.astype(BF16),
        bst_re=put_b(bst.real).astype(BF16), bst_im=put_b(bst.imag).astype(BF16),
        cst_re=put_c(w.real).astype(BF16), cst_im=put_c(-w.imag).astype(BF16),
        lam_re=lam_l.real, lam_im=lam_l.imag, dsk=d_skip.reshape(ns, 1, LANES))


def _s5_kernel(u_ref, m_ref, bre_ref, bim_ref, cre_ref, cim_ref, lre_ref, lim_ref, dsk_ref, o_ref,
               a_sc, s_re_sc, s_im_sc, in_re_sc, in_im_sc, carry_sc, *, rows, tiles_per_batch):
    L = S5_CHUNK

    @pl.when(pl.program_id(1) % tiles_per_batch == 0)
    def _():
        carry_sc[...] = jnp.zeros_like(carry_sc)

    for j in range(L):
        a_sc[:, j * LANES:(j + 1) * LANES] = u_ref[pl.ds(j, rows, stride=L), :].astype(BF16)

    s_re_sc[...] = _dot(a_sc[...], bre_ref[0])
    s_im_sc[...] = _dot(a_sc[...], bim_ref[0])
    lre = lre_ref[0]
    lim = lim_ref[0]

    def body(c, carry):
        sre, sim = carry
        in_re_sc[pl.ds(c, 1), :] = sre
        in_im_sc[pl.ds(c, 1), :] = sim
        return (lre * sre - lim * sim + s_re_sc[pl.ds(c, 1), :],
                lre * sim + lim * sre + s_im_sc[pl.ds(c, 1), :])

    sre, sim = lax.fori_loop(0, rows, body, (carry_sc[0:1, :], carry_sc[1:2, :]))
    carry_sc[0:1, :] = sre
    carry_sc[1:2, :] = sim

    sre = in_re_sc[...].astype(BF16)
    sim = in_im_sc[...].astype(BF16)
    pairw = 2 * LANES
    for b in range(L // 2):
        cols = slice(b * pairw, (b + 1) * pairw)
        kk = (b + 1) * pairw
        y2 = (_dot(a_sc[:, :kk], m_ref[0, :kk, cols])
              + _dot(sre, cre_ref[0, :, cols]) + _dot(sim, cim_ref[0, :, cols]))
        for h in range(2):
            t = 2 * b + h
            y = y2[:, h * LANES:(h + 1) * LANES] + dsk_ref[0] * u_ref[pl.ds(t, rows, stride=L), :]
            inner = math.sqrt(2.0 / math.pi) * (y + 0.044715 * (y * y * y))
            o_ref[pl.ds(t, rows, stride=L), :] = 0.5 * y * (1.0 + jnp.tanh(inner))


def _glu_residual_kernel(a_ref, w_ref, r_ref, o_ref, *, n):
    a = a_ref[...].astype(BF16)
    val = _dot(a, w_ref[:, :n])
    gate = _dot(a, w_ref[:, n:])
    o_ref[...] = r_ref[...] + val * _sigmoid(gate)


def _s5_mixer(x2, g, a_re, a_im, log_dt, b_re, b_im, c_re, c_im, d_skip, w_glu, seq):
    t, d = x2.shape
    L = S5_CHUNK
    G, K = d_skip.shape
    tab = _s5_tables(a_re, a_im, log_dt, b_re, b_im, c_re, c_im, d_skip)
    u = _rmsnorm(x2, g, F32)
    ns = d // LANES
    sw = (LANES // K) * S5_STATE
    steps = min(S5_STEPS, seq)
    rows = steps // L
    lw = L * LANES
    kern = functools.partial(_s5_kernel, rows=rows, tiles_per_batch=seq // steps)
    slab = lambda shape: pl.BlockSpec(shape, lambda s, i: (s, 0, 0))
    ya = pl.pallas_call(
        kern,
        out_shape=jax.ShapeDtypeStruct((t, d), F32),
        grid=(ns, t // steps),
        in_specs=[pl.BlockSpec((steps, LANES), lambda s, i: (i, s)),
                  slab((1, lw, lw)), slab((1, lw, sw)), slab((1, lw, sw)),
                  slab((1, sw, lw)), slab((1, sw, lw)),
                  slab((1, 1, sw)), slab((1, 1, sw)), slab((1, 1, LANES))],
        out_specs=pl.BlockSpec((steps, LANES), lambda s, i: (i, s)),
        scratch_shapes=[pltpu.VMEM((rows, lw), BF16)] + [pltpu.VMEM((rows, sw), F32)] * 4
                       + [pltpu.VMEM((SUBLANES, sw), F32)],
        compiler_params=_cparams(("parallel", "arbitrary"), 40),
        name="s5_scan",
    )(u, tab["m"], tab["bst_re"], tab["bst_im"], tab["cst_re"], tab["cst_im"],
      tab["lam_re"], tab["lam_im"], tab["dsk"])
    tm = min(512, t)
    return pl.pallas_call(
        functools.partial(_glu_residual_kernel, n=d),
        out_shape=jax.ShapeDtypeStruct((t, d), F32),
        grid=(t // tm,),
        in_specs=[pl.BlockSpec((tm, d), lambda i: (i, 0)),
                  _resident((d, 2 * d), lambda i: (0, 0)),
                  pl.BlockSpec((tm, d), lambda i: (i, 0))],
        out_specs=pl.BlockSpec((tm, d), lambda i: (i, 0)),
        compiler_params=_cparams(("parallel",), 40),
        name="glu_residual",
    )(ya, w_glu.astype(BF16), x2)


def kernel(x, norm_mix, norm_ffn, norm_final, attn_w_qkv, attn_w_o, attn_lambda_q1, attn_lambda_k1, attn_lambda_q2, attn_lambda_k2, attn_sub_gain, ssm_w_in, ssm_conv_w, ssm_conv_b, ssm_dt_bias, ssm_a_log, ssm_d, ssm_norm_gain, ssm_w_out, s5_a_re, s5_a_im, s5_log_dt, s5_b_re, s5_b_im, s5_c_re, s5_c_im, s5_d, s5_w_glu, ffn_w_up, ffn_conv_w, ffn_conv_b, ffn_w_down):
    batch, seq, d = x.shape
    depth = norm_mix.shape[0]
    x2 = x.reshape(batch * seq, d)
    for i in range(depth):
        kind = i % N_MIXERS
        j = i // N_MIXERS
        if kind == 0:
            lambda_init = 0.8 - 0.6 * math.exp(-0.3 * i)
            qkv = _qkv_proj(x2, norm_mix[i], attn_w_qkv[j], seq)
            o = _diff_attention_core(qkv, attn_lambda_q1[j], attn_lambda_k1[j], attn_lambda_q2[j],
                                     attn_lambda_k2[j], attn_sub_gain[j], lambda_init, batch, seq)
            x2 = _matmul_residual(o, attn_w_o[j].astype(BF16), x2)
        elif kind == 1:
            x2 = _mamba2_mixer(x2, norm_mix[i], ssm_w_in[j], ssm_conv_w[j], ssm_conv_b[j], ssm_dt_bias[j],
                               ssm_a_log[j], ssm_d[j], ssm_norm_gain[j], ssm_w_out[j], seq)
        else:
            x2 = _s5_mixer(x2, norm_mix[i], s5_a_re[j], s5_a_im[j], s5_log_dt[j], s5_b_re[j], s5_b_im[j],
                           s5_c_re[j], s5_c_im[j], s5_d[j], s5_w_glu[j], seq)
        x2 = _conv_ffn(x2, norm_ffn[i], ffn_w_up[i], ffn_conv_w[i], ffn_conv_b[i], ffn_w_down[i], seq)
    return _rmsnorm(x2, norm_final, F32).reshape(batch, seq, d)
```

```python
import functools
import math

import jax
import jax.numpy as jnp
from jax import lax
from jax.experimental import pallas as pl
from jax.experimental.pallas import tpu as pltpu

F32 = jnp.float32
BF16 = jnp.bfloat16

EPS = 1e-6
N_MIXERS = 3
CHUNK = 64
DA_HEADS = 8
DA_HEAD_DIM = 64
DA_VALUE_DIM = 2 * DA_HEAD_DIM
ROPE_THETA = 500000.0
ROPE_DIM = DA_HEAD_DIM // 4
SSM_HEAD_DIM = 64
SSM_GROUPS = 8
SSM_STATE = 128
SSM_CONV = 4
SSD_CHUNK = 128
S5_GROUP = 16
S5_STATE = 64
S5_CHUNK = 8
S5_STEPS = 2048
ATTN_UNROLL = 4
FFN_CONV = 3
FFN_COLS = 256
FFN_ROWS = 512
LANES = 128
SUBLANES = 8
MIB = 1024 * 1024


def _cparams(sem, vmem_mib):
    return pltpu.CompilerParams(dimension_semantics=sem, vmem_limit_bytes=vmem_mib * MIB)


def _resident(shape, index_map):
    return pl.BlockSpec(shape, index_map, pipeline_mode=pl.Buffered(1))


def _rms_rows(x, g):
    ms = jnp.mean(x * x, axis=-1, keepdims=True)
    return x * lax.rsqrt(ms + EPS) * g


def _sigmoid(x):
    return 0.5 + 0.5 * jnp.tanh(0.5 * x)


def _silu(x):
    h = 0.5 * x
    return h + h * jnp.tanh(h)


def _idiv(v, n):
    return lax.shift_right_logical(v, int(math.log2(n)))


def _split3(x):
    hi = x.astype(BF16)
    r = x - hi.astype(F32)
    mid = r.astype(BF16)
    lo = (r - mid.astype(F32)).astype(BF16)
    return hi, mid, lo


def _dot(a, b):
    return jnp.dot(a, b, preferred_element_type=F32)


def _dot3_rhs01(x, e):
    hi, mid, lo = _split3(x)
    return _dot(hi, e) + _dot(mid, e) + _dot(lo, e)


def _dot3_lhs01(e, x):
    hi, mid, lo = _split3(x)
    return _dot(e, hi) + _dot(e, mid) + _dot(e, lo)


def _norm_kernel(x_ref, g_ref, o_ref):
    o_ref[...] = _rms_rows(x_ref[...], g_ref[...]).astype(o_ref.dtype)


def _rmsnorm(x2, g, out_dtype):
    t, d = x2.shape
    tm = min(512, t)
    return pl.pallas_call(
        _norm_kernel,
        out_shape=jax.ShapeDtypeStruct((t, d), out_dtype),
        grid=(t // tm,),
        in_specs=[pl.BlockSpec((tm, d), lambda i: (i, 0)),
                  pl.BlockSpec((1, d), lambda i: (0, 0))],
        out_specs=pl.BlockSpec((tm, d), lambda i: (i, 0)),
        compiler_params=_cparams(("parallel",), 32),
        name="rmsnorm",
    )(x2, g.reshape(1, d))


def _inproj_conv_kernel(x_ref, g_ref, w_ref, wdt_ref, cw_ref, cb_ref, o_ref, dt_ref, hn_sc, hbuf_sc, halo_sc,
                        *, tm, tn, n_conv, steps_per_batch):
    taps = cw_ref.shape[0]

    @pl.when(pl.program_id(0) % steps_per_batch == 0)
    def _():
        halo_sc[...] = jnp.zeros_like(halo_sc)

    hn_sc[...] = _rms_rows(x_ref[...], g_ref[...]).astype(BF16)
    dt_ref[...] = _dot(hn_sc[...], wdt_ref[...])
    for j in range(w_ref.shape[1] // tn):
        cols = slice(j * tn, (j + 1) * tn)
        y = _dot(hn_sc[...], w_ref[:, cols])
        if j * tn < n_conv:
            buf = hbuf_sc.at[j % 2]
            buf[0:SUBLANES, :] = halo_sc[j]
            buf[SUBLANES:SUBLANES + tm, :] = y
            halo_sc[j] = buf[tm:tm + SUBLANES, :]
            y = cb_ref[:, cols]
            for k in range(taps):
                off = SUBLANES - (taps - 1) + k
                y = y + cw_ref[k:k + 1, cols] * buf[off:off + tm, :]
            y = _silu(y)
        o_ref[:, cols] = y.astype(o_ref.dtype)


def _inproj_conv(x2, g, w, w_dt, conv_w, conv_b, seq, tn, tm):
    t, d = x2.shape
    n = w.shape[1]
    ndt = w_dt.shape[1]
    n_conv = conv_w.shape[1]
    tm = min(tm, seq)
    kern = functools.partial(_inproj_conv_kernel, tm=tm, tn=tn, n_conv=n_conv, steps_per_batch=seq // tm)
    return pl.pallas_call(
        kern,
        out_shape=(jax.ShapeDtypeStruct((t, n), BF16), jax.ShapeDtypeStruct((t, ndt), F32)),
        grid=(t // tm,),
        in_specs=[pl.BlockSpec((tm, d), lambda i: (i, 0)),
                  _resident((1, d), lambda i: (0, 0)),
                  _resident((d, n), lambda i: (0, 0)),
                  _resident((d, ndt), lambda i: (0, 0)),
                  _resident(conv_w.shape, lambda i: (0, 0)),
                  _resident((1, n_conv), lambda i: (0, 0))],
        out_specs=(pl.BlockSpec((tm, n), lambda i: (i, 0)), pl.BlockSpec((tm, ndt), lambda i: (i, 0))),
        scratch_shapes=[pltpu.VMEM((tm, d), BF16),
                        pltpu.VMEM((2, tm + SUBLANES, tn), F32),
                        pltpu.VMEM((n_conv // tn, SUBLANES, tn), F32)],
        compiler_params=_cparams(("arbitrary",), 48),
        name="inproj_conv",
    )(x2, g.reshape(1, d), w, w_dt, conv_w, conv_b.reshape(1, -1))


def _matmul_residual_kernel(a_ref, w_ref, r_ref, o_ref):
    o_ref[...] = r_ref[...] + _dot(a_ref[...], w_ref[...])


def _matmul_residual(a, w, res):
    t, k = a.shape
    n = w.shape[1]
    tm = min(512, t)
    return pl.pallas_call(
        _matmul_residual_kernel,
        out_shape=jax.ShapeDtypeStruct((t, n), F32),
        grid=(t // tm,),
        in_specs=[pl.BlockSpec((tm, k), lambda i: (i, 0)),
                  _resident((k, n), lambda i: (0, 0)),
                  pl.BlockSpec((tm, n), lambda i: (i, 0))],
        out_specs=pl.BlockSpec((tm, n), lambda i: (i, 0)),
        compiler_params=_cparams(("parallel",), 40),
        name="matmul_residual",
    )(a, w, res)


def _ffn_kernel(x_ref, g_ref, wup_ref, cw_ref, cb_ref, wdn_ref, o_ref,
                hn_sc, hbuf_sc, halo_sc, act_sc, *, tm, ffn, steps_per_batch):
    cols = FFN_COLS
    nchunk = ffn // cols

    @pl.when(pl.program_id(0) % steps_per_batch == 0)
    def _():
        halo_sc[...] = jnp.zeros_like(halo_sc)

    hn_sc[...] = _rms_rows(x_ref[...], g_ref[...]).astype(BF16)

    def up(c):
        lo = c * cols
        buf = hbuf_sc.at[c % 2]
        hn = hn_sc[...]
        buf[0:SUBLANES, :] = halo_sc[c]
        buf[SUBLANES:SUBLANES + tm, 0:cols] = _dot(hn, wup_ref[:, lo:lo + cols])
        buf[SUBLANES:SUBLANES + tm, cols:2 * cols] = _dot(hn, wup_ref[:, ffn + lo:ffn + lo + cols])
        halo_sc[c] = buf[tm:tm + SUBLANES, :]

    def activate(c):
        lo = c * cols
        buf = hbuf_sc.at[c % 2]
        w3 = jnp.concatenate([cw_ref[:, lo:lo + cols], cw_ref[:, ffn + lo:ffn + lo + cols]], axis=1)
        y = jnp.concatenate([cb_ref[:, lo:lo + cols], cb_ref[:, ffn + lo:ffn + lo + cols]], axis=1)
        for k in range(FFN_CONV):
            off = SUBLANES - (FFN_CONV - 1) + k
            y = y + w3[k:k + 1, :] * buf[off:off + tm, :]
        act_sc[:, lo:lo + cols] = (_silu(y[:, :cols]) * y[:, cols:]).astype(BF16)

    up(0)
    for c in range(nchunk):
        if c + 1 < nchunk:
            up(c + 1)
        activate(c)
    o_ref[...] = x_ref[...] + _dot(act_sc[...], wdn_ref[...])


def _conv_ffn(x2, g, w_up, conv_w, conv_b, w_down, seq):
    t, d = x2.shape
    ffn = w_down.shape[0]
    tm = min(FFN_ROWS, seq)
    kern = functools.partial(_ffn_kernel, tm=tm, ffn=ffn, steps_per_batch=seq // tm)
    return pl.pallas_call(
        kern,
        out_shape=jax.ShapeDtypeStruct((t, d), F32),
        grid=(t // tm,),
        in_specs=[pl.BlockSpec((tm, d), lambda i: (i, 0)),
                  _resident((1, d), lambda i: (0, 0)),
                  _resident((d, 2 * ffn), lambda i: (0, 0)),
                  _resident((FFN_CONV, 2 * ffn), lambda i: (0, 0)),
                  _resident((1, 2 * ffn), lambda i: (0, 0)),
                  _resident((ffn, d), lambda i: (0, 0))],
        out_specs=pl.BlockSpec((tm, d), lambda i: (i, 0)),
        scratch_shapes=[pltpu.VMEM((tm, d), BF16),
                        pltpu.VMEM((2, tm + SUBLANES, 2 * FFN_COLS), F32),
                        pltpu.VMEM((ffn // FFN_COLS, SUBLANES, 2 * FFN_COLS), F32),
                        pltpu.VMEM((tm, ffn), BF16)],
        compiler_params=_cparams(("arbitrary",), 48),
        name="conv_ffn",
    )(x2, g.reshape(1, d), w_up.astype(BF16), conv_w, conv_b.reshape(1, -1), w_down.astype(BF16))


def _qkv_kernel(x_ref, g_ref, w_ref, wr_ref, cos_ref, sin_ref, o_ref, hn_sc, *, tn):
    hn_sc[...] = _rms_rows(x_ref[...], g_ref[...]).astype(BF16)
    reps = tn // LANES
    n_rope = wr_ref.shape[1]
    for j in range(w_ref.shape[1] // tn):
        cols = slice(j * tn, (j + 1) * tn)
        y = _dot(hn_sc[...], w_ref[:, cols])
        if j * tn < n_rope:
            cos = jnp.concatenate([cos_ref[...]] * reps, axis=1)
            sin = jnp.concatenate([sin_ref[...]] * reps, axis=1)
            y = y * cos + _dot(hn_sc[...], wr_ref[:, cols]) * sin
        o_ref[:, cols] = y.astype(o_ref.dtype)


def _rope_tables(seq):
    half = ROPE_DIM // 2
    pos = jnp.arange(seq, dtype=F32)
    inv_freq = ROPE_THETA ** (-jnp.arange(0, ROPE_DIM, 2, dtype=F32) / ROPE_DIM)
    ang = pos[:, None] * inv_freq[None, :]
    cos, sin = jnp.cos(ang), jnp.sin(ang)
    lane = jnp.arange(LANES) % DA_HEAD_DIM
    idx = lane % half
    cos_l = jnp.where(lane[None, :] < ROPE_DIM, cos[:, idx], 1.0)
    sin_l = jnp.where(lane[None, :] < ROPE_DIM, sin[:, idx], 0.0)
    return cos_l, sin_l


def _rope_partner_weight(w):
    half = ROPE_DIM // 2
    col = jnp.arange(w.shape[1])
    lane = col % DA_HEAD_DIM
    partner = jnp.where(lane < half, col + half, col - half)
    sign = jnp.where(lane < half, -1.0, jnp.where(lane < ROPE_DIM, 1.0, 0.0))
    return w[:, jnp.clip(partner, 0, w.shape[1] - 1)] * sign[None, :]


def _qkv_proj(x2, g, w_qkv, seq):
    t, d = x2.shape
    width = w_qkv.shape[1] // 3
    scale = DA_HEAD_DIM ** -0.5 * math.log2(math.e)
    w = jnp.concatenate([w_qkv[:, :width] * scale, w_qkv[:, width:]], axis=1).astype(BF16)
    wr = _rope_partner_weight(w[:, :2 * width])
    n = w.shape[1]
    tm = min(512, seq)
    tn = 512
    cos_l, sin_l = _rope_tables(seq)
    spb = seq // tm
    tab = pl.BlockSpec((tm, LANES), lambda i: (i % spb, 0))
    return pl.pallas_call(
        functools.partial(_qkv_kernel, tn=tn),
        out_shape=jax.ShapeDtypeStruct((t, n), BF16),
        grid=(t // tm,),
        in_specs=[pl.BlockSpec((tm, d), lambda i: (i, 0)),
                  _resident((1, d), lambda i: (0, 0)),
                  _resident((d, n), lambda i: (0, 0)),
                  _resident((d, 2 * width), lambda i: (0, 0)),
                  tab, tab],
        out_specs=pl.BlockSpec((tm, n), lambda i: (i, 0)),
        scratch_shapes=[pltpu.VMEM((tm, d), BF16)],
        compiler_params=_cparams(("parallel",), 40),
        name="qkv_rope",
    )(x2, g.reshape(1, d), w, wr, cos_l, sin_l)


def _attn_kernel(q_ref, k_ref, v_ref, lq1_ref, lk1_ref, lq2_ref, lk2_ref, gain_ref, o_ref,
                 vt_sc, qt_sc, s_sc, mx_sc, acc_sc, m_sc, l_sc, *, tq, seq, lambda_init):
    qi = pl.program_id(2)
    nkv = seq // tq

    @pl.when(qi == 0)
    def _():
        def tr(t, carry):
            st = pl.multiple_of(t * tq, tq)
            vt_sc[t] = v_ref[pl.ds(st, tq), :].astype(F32).T.astype(BF16)
            return carry
        lax.fori_loop(0, nkv, tr, 0)

    qt = q_ref[...].astype(F32).T
    row = lax.broadcasted_iota(jnp.int32, qt.shape, 0)
    qt_sc[0] = jnp.where(row < DA_HEAD_DIM, qt, 0.0).astype(BF16)
    qt_sc[1] = jnp.where(row >= DA_HEAD_DIM, qt, 0.0).astype(BF16)

    m_sc[...] = jnp.full_like(m_sc, -jnp.inf)
    l_sc[...] = jnp.zeros_like(l_sc)
    acc_sc[...] = jnp.zeros_like(acc_sc)

    def scores(t, slot, masked):
        st = pl.multiple_of(t * tq, tq)
        kt = k_ref[pl.ds(st, tq), :]
        if masked:
            kc = _idiv(lax.broadcasted_iota(jnp.int32, (tq, tq), 0), CHUNK)
            qc = _idiv(lax.broadcasted_iota(jnp.int32, (tq, tq), 1), CHUNK)
            keep = kc <= qc
        for c in range(2):
            s = _dot(kt, qt_sc[c])
            if masked:
                s = jnp.where(keep, s, -jnp.inf)
            s_sc[slot, c] = s
            mx_sc[slot, c] = jnp.max(s, axis=0, keepdims=True)

    def accumulate(t, slot):
        vt = vt_sc[t]
        for c in range(2):
            m_prev = m_sc[c]
            m_new = jnp.maximum(m_prev, mx_sc[slot, c])
            alpha = jnp.exp2(m_prev - m_new)
            p = jnp.exp2(s_sc[slot, c] - m_new)
            l_sc[c] = alpha * l_sc[c] + jnp.sum(p, axis=0, keepdims=True)
            acc_sc[c] = alpha * acc_sc[c] + _dot(vt, p.astype(BF16))
            m_sc[c] = m_new

    scores(qi, 0, True)

    def run(width):
        def body(u, carry):
            for j in range(width):
                t = width * u + j
                scores(t, (j + 1) % 2, False)
                accumulate(jnp.where(t == 0, qi, t - 1), j % 2)
            return carry
        return body

    nquad = qi // ATTN_UNROLL
    npair = qi // 2
    lax.fori_loop(0, nquad, run(ATTN_UNROLL), 0)
    lax.fori_loop(nquad * (ATTN_UNROLL // 2), npair, run(2), 0)
    odd = qi % 2 == 1

    @pl.when(odd)
    def _():
        scores(qi - 1, 1, False)
        accumulate(jnp.where(qi == 1, qi, qi - 2), 0)
        accumulate(qi - 1, 1)

    @pl.when(jnp.logical_not(odd))
    def _():
        accumulate(jnp.where(qi == 0, qi, qi - 1), 0)

    lam =(jnp.exp(jnp.sum(lq1_ref[...] * lk1_ref[...], axis=-1, keepdims=True))
           - jnp.exp(jnp.sum(lq2_ref[...] * lk2_ref[...], axis=-1, keepdims=True)) + lambda_init)
    o = acc_sc[0] / l_sc[0] - lam * (acc_sc[1] / l_sc[1])
    ms = jnp.mean(o * o, axis=0, keepdims=True)
    on = (o * lax.rsqrt(ms + EPS)).T
    o_ref[...] = (on * gain_ref[...] * (1.0 - lambda_init)).astype(o_ref.dtype)


def _diff_attention_core(qkv, lq1, lk1, lq2, lk2, sub_gain, lambda_init, batch, seq):
    width = DA_HEADS * DA_VALUE_DIM
    qkv3 = qkv.reshape(batch, seq, 3 * width)
    tq = min(512, seq)
    kern = functools.partial(_attn_kernel, tq=tq, seq=seq, lambda_init=lambda_init)
    vec = pl.BlockSpec((1, DA_HEAD_DIM), lambda b, h, i: (0, 0))
    return pl.pallas_call(
        kern,
        out_shape=jax.ShapeDtypeStruct((batch, seq, width), BF16),
        grid=(batch, DA_HEADS, seq // tq),
        in_specs=[pl.BlockSpec((None, tq, DA_VALUE_DIM), lambda b, h, i: (b, i, h)),
                  pl.BlockSpec((None, seq, DA_VALUE_DIM), lambda b, h, i: (b, 0, DA_HEADS + h)),
                  pl.BlockSpec((None, seq, DA_VALUE_DIM), lambda b, h, i: (b, 0, 2 * DA_HEADS + h)),
                  vec, vec, vec, vec,
                  pl.BlockSpec((1, DA_VALUE_DIM), lambda b, h, i: (0, 0))],
        out_specs=pl.BlockSpec((None, tq, DA_VALUE_DIM), lambda b, h, i: (b, i, h)),
        scratch_shapes=[pltpu.VMEM((seq // tq, DA_VALUE_DIM, tq), BF16),
                        pltpu.VMEM((2, DA_VALUE_DIM, tq), BF16),
                        pltpu.VMEM((2, 2, tq, tq), F32),
                        pltpu.VMEM((2, 2, 1, tq), F32),
                        pltpu.VMEM((2, DA_VALUE_DIM, tq), F32),
                        pltpu.VMEM((2, 1, tq), F32),
                        pltpu.VMEM((2, 1, tq), F32)],
        compiler_params=_cparams(("parallel", "parallel", "arbitrary"), 48),
        name="diff_attention",
    )(qkv3, qkv3, qkv3, lq1.reshape(1, -1), lk1.reshape(1, -1), lq2.reshape(1, -1),
      lk2.reshape(1, -1), sub_gain.reshape(1, -1)).reshape(batch * seq, width)


def _ssd_kernel(xbc_ref, z_ref, dt_ref, dtb_ref, alog_ref, dsk_ref, ng_ref,
                tril_ref, e128_ref, e64_ref, y_ref, state_sc, *, steps_per_batch, inner):
    L = SSD_CHUNK
    G = SSM_GROUPS
    N = SSM_STATE
    gw = inner // G
    R = gw // SSM_HEAD_DIM

    @pl.when(pl.program_id(0) % steps_per_batch == 0)
    def _():
        state_sc[...] = jnp.zeros_like(state_sc)

    xbc = xbc_ref[...].astype(F32)

    x = dt_ref[...] + dtb_ref[...]
    dt = jnp.maximum(x, 0.0) + jnp.log(1.0 + jnp.exp(-jnp.abs(x)))
    adt = dt * (-jnp.exp(alog_ref[...]))

    acum = _dot3_lhs01(tril_ref[...], adt)
    acol128 = _dot3_rhs01(acum, e128_ref[...])
    acol64 = _dot3_rhs01(acum, e64_ref[...])
    dt64 = _dot3_rhs01(dt, e64_ref[...])

    ri = lax.broadcasted_iota(jnp.int32, acol128.shape, 0)
    ci = lax.broadcasted_iota(jnp.int32, acol128.shape, 1) & (L - 1)
    arow = jnp.sum(jnp.where(ri == ci, acol128, 0.0), axis=0, keepdims=True)
    decay_in = jnp.exp(jnp.where(ci <= ri, acol128 - arow, -jnp.inf))

    xs = xbc[:, :inner]
    xdt = xs * dt64
    last = acol64[L - 1:L, :]
    xdo = (xdt * jnp.exp(last - acol64)).astype(BF16)
    xdt = xdt.astype(BF16)
    e_a = jnp.exp(acol64)
    e_last = jnp.exp(last)

    rr = _idiv(lax.broadcasted_iota(jnp.int32, (R * L, gw), 0), L)
    cc = _idiv(lax.broadcasted_iota(jnp.int32, (R * L, gw), 1), SSM_HEAD_DIM)
    blockdiag = rr == cc

    for g in range(G):
        bg = xbc[:, inner + g * N: inner + (g + 1) * N]
        cg = xbc[:, inner + G * N + g * N: inner + G * N + (g + 1) * N].astype(BF16)
        bgt = bg.T.astype(BF16)
        cb = _dot(cg, bgt)
        cbt = jnp.concatenate([cb] * R, axis=1)
        mg = (cbt * decay_in[:, g * R * L:(g + 1) * R * L]).astype(BF16)
        xg = xdt[:, g * gw:(g + 1) * gw]
        xbd = jnp.where(blockdiag, jnp.concatenate([xg] * R, axis=0), jnp.zeros((), BF16))
        y = _dot(mg, xbd)
        st = state_sc[g]
        y = y + _dot(cg, st.astype(BF16)) * e_a[:, g * gw:(g + 1) * gw]
        state_sc[g] = st * e_last[:, g * gw:(g + 1) * gw] + _dot(bgt, xdo[:, g * gw:(g + 1) * gw])
        y = y + dsk_ref[:, g * gw:(g + 1) * gw] * xs[:, g * gw:(g + 1) * gw]
        y = y * _silu(z_ref[:, g * gw:(g + 1) * gw].astype(F32))
        ms = jnp.mean(y * y, axis=-1, keepdims=True)
        y_ref[:, g * gw:(g + 1) * gw] = (y * lax.rsqrt(ms + EPS) * ng_ref[:, g * gw:(g + 1) * gw]).astype(y_ref.dtype)


def _mamba2_mixer(x2, g, w_in, conv_w, conv_b, dt_bias, a_log, d_skip, norm_gain, w_out, seq):
    t, d = x2.shape
    nh = dt_bias.shape[0]
    inner = nh * SSM_HEAD_DIM
    conv_dim = conv_w.shape[1]
    L = SSD_CHUNK
    w_xz = jnp.concatenate([w_in[:, inner:inner + conv_dim], w_in[:, :inner]], axis=1).astype(BF16)
    w_dt = jnp.pad(w_in[:, inner + conv_dim:], ((0, 0), (0, LANES - nh))).astype(BF16)
    xz, dtr = _inproj_conv(x2, g, w_xz, w_dt, conv_w, conv_b, seq, 1024, 512)

    pad1 = lambda v: jnp.pad(v, (0, LANES - nh)).reshape(1, LANES)
    rows = jnp.arange(L)
    tril = (rows[None, :] <= rows[:, None]).astype(BF16)
    head = jnp.arange(LANES)
    e128 = (head[:, None] == (jnp.arange(nh * L) // L)[None, :]).astype(BF16)
    e64 = (head[:, None] == (jnp.arange(inner) // SSM_HEAD_DIM)[None, :]).astype(BF16)
    dsk = jnp.repeat(d_skip, SSM_HEAD_DIM).reshape(1, inner)

    nblk_x = conv_dim // inner
    kern = functools.partial(_ssd_kernel, steps_per_batch=seq // L, inner=inner)
    const = lambda shape: _resident(shape, lambda i: (0, 0))
    y = pl.pallas_call(
        kern,
        out_shape=jax.ShapeDtypeStruct((t, inner), BF16),
        grid=(t // L,),
        in_specs=[pl.BlockSpec((L, conv_dim), lambda i: (i, 0)),
                  pl.BlockSpec((L, inner), lambda i: (i, nblk_x)),
                  pl.BlockSpec((L, LANES), lambda i: (i, 0)),
                  const((1, LANES)), const((1, LANES)),
                  const((1, inner)), const((1, inner)),
                  const((L, L)), const((LANES, nh * L)), const((LANES, inner))],
        out_specs=pl.BlockSpec((L, inner), lambda i: (i, 0)),
        scratch_shapes=[pltpu.VMEM((SSM_GROUPS, SSM_STATE, inner // SSM_GROUPS), F32)],
        compiler_params=_cparams(("arbitrary",), 48),
        name="ssd_scan",
    )(xz, xz, dtr, pad1(dt_bias), pad1(a_log), dsk,
      norm_gain.reshape(1, -1), tril, e128, e64)
    return _matmul_residual(y, w_out.astype(BF16), x2)


def _s5_tables(a_re, a_im, log_dt, b_re, b_im, c_re, c_im, d_skip):
    L = S5_CHUNK
    G, P = a_re.shape
    K = b_re.shape[-1]
    lam = lax.complex(jnp.minimum(a_re, -1e-4), a_im)
    delta = jnp.exp(log_dt)[:, None]
    lam_bar = jnp.exp(lam * delta)
    b_bar = ((lam_bar - 1.0) / lam)[..., None] * lax.complex(b_re, b_im)
    c = lax.complex(c_re, c_im)
    tau = jnp.arange(L + 1, dtype=F32)
    pw = jnp.exp(tau[:, None, None] * (lam * delta)[None])

    gs = LANES // K
    ns = G // gs
    eye = jnp.eye(gs, dtype=F32)
    kt = jnp.einsum("gkp,tgp,gpi->gtki", c, pw[:L], b_bar).real
    lag = jnp.arange(L)[None, :] - jnp.arange(L)[:, None]
    blk = jnp.einsum("sgtoi,gh->stgiho", kt.reshape(ns, gs, L, K, K), eye).reshape(ns, L, LANES, LANES)
    m = jnp.where((lag >= 0)[None, :, :, None, None], blk[:, jnp.maximum(lag, 0)], 0.0)
    m = m.transpose(0, 1, 3, 2, 4).reshape(ns, L * LANES, L * LANES)

    bst = pw[:L][::-1].transpose(1, 0, 2)[:, :, None, :] * b_bar.transpose(0, 2, 1)[:, None, :, :]
    bst = bst.reshape(ns, gs, L, K, P)
    put_b = lambda v: jnp.einsum("sgjip,gh->sjgihp", v, eye).reshape(ns, L * LANES, gs * P)
    w = c.transpose(0, 2, 1)[:, :, None, :] * pw[1:].transpose(1, 2, 0)[:, :, :, None]
    w = w.reshape(ns, gs, P, L, K)
    put_c = lambda v: jnp.einsum("sgpto,gh->sgptho", v, eye).reshape(ns, gs * P, L * LANES)
    lam_l = pw[L].reshape(ns, 1, gs * P)
    return dict(
        m=m.astype(BF16),
        bst_re=put_b(bst.real).astype(BF16), bst_im=put_b(bst.imag).astype(BF16),
        cst_re=put_c(w.real).astype(BF16), cst_im=put_c(-w.imag).astype(BF16),
        lam_re=lam_l.real, lam_im=lam_l.imag, dsk=d_skip.reshape(ns, 1, LANES))


def _s5_kernel(u_ref, m_ref, bre_ref, bim_ref, cre_ref, cim_ref, lre_ref, lim_ref, dsk_ref, o_ref,
               a_sc, s_re_sc, s_im_sc, in_re_sc, in_im_sc, carry_sc, *, rows, batch):
    L = S5_CHUNK

    @pl.when(pl.program_id(1) == 0)
    def _():
        carry_sc[...] = jnp.zeros_like(carry_sc)

    for b in range(batch):
        for j in range(L):
            a_sc[b * rows:(b + 1) * rows, j * LANES:(j + 1) * LANES] = (
                u_ref[b, pl.ds(j, rows, stride=L), :].astype(BF16))

    s_re_sc[...] = _dot(a_sc[...], bre_ref[0])
    s_im_sc[...] = _dot(a_sc[...], bim_ref[0])
    lre = lre_ref[0]
    lim = lim_ref[0]

    def body(c, carry):
        out = []
        for b in range(batch):
            sre, sim = carry[b]
            r = b * rows + c
            in_re_sc[pl.ds(r, 1), :] = sre
            in_im_sc[pl.ds(r, 1), :] = sim
            out.append((lre * sre - lim * sim + s_re_sc[pl.ds(r, 1), :],
                        lre * sim + lim * sre + s_im_sc[pl.ds(r, 1), :]))
        return tuple(out)

    init = tuple((carry_sc[2 * b:2 * b + 1, :], carry_sc[2 * b + 1:2 * b + 2, :]) for b in range(batch))
    last = lax.fori_loop(0, rows, body, init)
    for b in range(batch):
        carry_sc[2 * b:2 * b + 1, :] = last[b][0]
        carry_sc[2 * b + 1:2 * b + 2, :] = last[b][1]

    sre = in_re_sc[...].astype(BF16)
    sim = in_im_sc[...].astype(BF16)
    pairw = 2 * LANES
    for p in range(L // 2):
        cols = slice(p * pairw, (p + 1) * pairw)
        kk = (p + 1) * pairw
        y2 = (_dot(a_sc[:, :kk], m_ref[0, :kk, cols])
              + _dot(sre, cre_ref[0, :, cols]) + _dot(sim, cim_ref[0, :, cols]))
        for h in range(2):
            t = 2 * p + h
            for b in range(batch):
                y = (y2[b * rows:(b + 1) * rows, h * LANES:(h + 1) * LANES]
                     + dsk_ref[0] * u_ref[b, pl.ds(t, rows, stride=L), :])
                inner = math.sqrt(2.0 / math.pi) * (y + 0.044715 * (y * y * y))
                o_ref[b, pl.ds(t, rows, stride=L), :] = 0.5 * y * (1.0 + jnp.tanh(inner))


def _glu_residual_kernel(a_ref, w_ref, r_ref, o_ref, *, n):
    a = a_ref[...].astype(BF16)
    val = _dot(a, w_ref[:, :n])
    gate = _dot(a, w_ref[:, n:])
    o_ref[...] = r_ref[...] + val * _sigmoid(gate)


def _s5_mixer(x2, g, a_re, a_im, log_dt, b_re, b_im, c_re, c_im, d_skip, w_glu, seq):
    t, d = x2.shape
    L = S5_CHUNK
    G, K = d_skip.shape
    tab = _s5_tables(a_re, a_im, log_dt, b_re, b_im, c_re, c_im, d_skip)
    u = _rmsnorm(x2, g, F32)
    ns = d // LANES
    sw = (LANES // K) * S5_STATE
    batch = t // seq
    assert 2 * batch <= SUBLANES
    steps = min(S5_STEPS, seq)
    rows = steps // L
    lw = L * LANES
    kern = functools.partial(_s5_kernel, rows=rows, batch=batch)
    slab = lambda shape: pl.BlockSpec(shape, lambda s, i: (s, 0, 0))
    ya = pl.pallas_call(
        kern,
        out_shape=jax.ShapeDtypeStruct((batch, seq, d), F32),
        grid=(ns, seq // steps),
        in_specs=[pl.BlockSpec((batch, steps, LANES), lambda s, i: (0, i, s)),
                  slab((1, lw, lw)), slab((1, lw, sw)), slab((1, lw, sw)),
                  slab((1, sw, lw)), slab((1, sw, lw)),
                  slab((1, 1, sw)), slab((1, 1, sw)), slab((1, 1, LANES))],
        out_specs=pl.BlockSpec((batch, steps, LANES), lambda s, i: (0, i, s)),
        scratch_shapes=[pltpu.VMEM((batch * rows, lw), BF16)] + [pltpu.VMEM((batch * rows, sw), F32)] * 4
                       + [pltpu.VMEM((SUBLANES, sw), F32)],
        compiler_params=_cparams(("parallel", "arbitrary"), 48),
        name="s5_scan",
    )(u.reshape(batch, seq, d), tab["m"], tab["bst_re"], tab["bst_im"], tab["cst_re"], tab["cst_im"],
      tab["lam_re"], tab["lam_im"], tab["dsk"]).reshape(t, d)
    tm = min(512, t)
    return pl.pallas_call(
        functools.partial(_glu_residual_kernel, n=d),
        out_shape=jax.ShapeDtypeStruct((t, d), F32),
        grid=(t // tm,),
        in_specs=[pl.BlockSpec((tm, d), lambda i: (i, 0)),
                  _resident((d, 2 * d), lambda i: (0, 0)),
                  pl.BlockSpec((tm, d), lambda i: (i, 0))],
        out_specs=pl.BlockSpec((tm, d), lambda i: (i, 0)),
        compiler_params=_cparams(("parallel",), 40),
        name="glu_residual",
    )(ya, w_glu.astype(BF16), x2)


def kernel(x, norm_mix, norm_ffn, norm_final, attn_w_qkv, attn_w_o, attn_lambda_q1, attn_lambda_k1, attn_lambda_q2, attn_lambda_k2, attn_sub_gain, ssm_w_in, ssm_conv_w, ssm_conv_b, ssm_dt_bias, ssm_a_log, ssm_d, ssm_norm_gain, ssm_w_out, s5_a_re, s5_a_im, s5_log_dt, s5_b_re, s5_b_im, s5_c_re, s5_c_im, s5_d, s5_w_glu, ffn_w_up, ffn_conv_w, ffn_conv_b, ffn_w_down):
    batch, seq, d = x.shape
    depth = norm_mix.shape[0]
    x2 = x.reshape(batch * seq, d)
    for i in range(depth):
        kind = i % N_MIXERS
        j = i // N_MIXERS
        if kind == 0:
            lambda_init = 0.8 - 0.6 * math.exp(-0.3 * i)
            qkv = _qkv_proj(x2, norm_mix[i], attn_w_qkv[j], seq)
            o = _diff_attention_core(qkv, attn_lambda_q1[j], attn_lambda_k1[j], attn_lambda_q2[j],
                                     attn_lambda_k2[j], attn_sub_gain[j], lambda_init, batch, seq)
            x2 = _matmul_residual(o, attn_w_o[j].astype(BF16), x2)
        elif kind == 1:
            x2 = _mamba2_mixer(x2, norm_mix[i], ssm_w_in[j], ssm_conv_w[j], ssm_conv_b[j], ssm_dt_bias[j],
                               ssm_a_log[j], ssm_d[j], ssm_norm_gain[j], ssm_w_out[j], seq)
        else:
            x2 = _s5_mixer(x2, norm_mix[i], s5_a_re[j], s5_a_im[j], s5_log_dt[j], s5_b_re[j], s5_b_im[j],
                           s5_c_re[j], s5_c_im[j], s5_d[j], s5_w_glu[j], seq)
        x2 = _conv_ffn(x2, norm_ffn[i], ffn_w_up[i], ffn_conv_w[i], ffn_conv_b[i], ffn_w_down[i], seq)
    return _rmsnorm(x2, norm_final, F32).reshape(batch, seq, d)
```

```python
import functools
import math

import jax
import jax.numpy as jnp
from jax import lax
from jax.experimental import pallas as pl
from jax.experimental.pallas import tpu as pltpu

F32 = jnp.float32
BF16 = jnp.bfloat16

EPS = 1e-6
N_MIXERS = 3
CHUNK = 64
DA_HEADS = 8
DA_HEAD_DIM = 64
DA_VALUE_DIM = 2 * DA_HEAD_DIM
ROPE_THETA = 500000.0
ROPE_DIM = DA_HEAD_DIM // 4
SSM_HEAD_DIM = 64
SSM_GROUPS = 8
SSM_STATE = 128
SSM_CONV = 4
SSD_CHUNK = 128
S5_GROUP = 16
S5_STATE = 64
S5_CHUNK = 8
S5_STEPS = 2048
ATTN_UNROLL = (4, 8)
FFN_CONV = 3
FFN_COLS = 256
FFN_ROWS = (512, 512, 512, 256)
LANES = 128
SUBLANES = 8
MIB = 1024 * 1024


def _cparams(sem, vmem_mib):
    return pltpu.CompilerParams(dimension_semantics=sem, vmem_limit_bytes=vmem_mib * MIB)


def _resident(shape, index_map):
    return pl.BlockSpec(shape, index_map, pipeline_mode=pl.Buffered(1))


def _rms_rows(x, g):
    ms = jnp.mean(x * x, axis=-1, keepdims=True)
    return x * lax.rsqrt(ms + EPS) * g


def _sigmoid(x):
    return 0.5 + 0.5 * jnp.tanh(0.5 * x)


def _silu(x):
    h = 0.5 * x
    return h + h * jnp.tanh(h)


def _idiv(v, n):
    return lax.shift_right_logical(v, int(math.log2(n)))


def _split3(x):
    hi = x.astype(BF16)
    r = x - hi.astype(F32)
    mid = r.astype(BF16)
    lo = (r - mid.astype(F32)).astype(BF16)
    return hi, mid, lo


def _dot(a, b):
    return jnp.dot(a, b, preferred_element_type=F32)


def _dot3_rhs01(x, e):
    hi, mid, lo = _split3(x)
    return _dot(hi, e) + _dot(mid, e) + _dot(lo, e)


def _dot3_lhs01(e, x):
    hi, mid, lo = _split3(x)
    return _dot(e, hi) + _dot(e, mid) + _dot(e, lo)


def _norm_kernel(x_ref, g_ref, o_ref):
    o_ref[...] = _rms_rows(x_ref[...], g_ref[...]).astype(o_ref.dtype)


def _rmsnorm(x2, g, out_dtype):
    t, d = x2.shape
    tm = min(512, t)
    return pl.pallas_call(
        _norm_kernel,
        out_shape=jax.ShapeDtypeStruct((t, d), out_dtype),
        grid=(t // tm,),
        in_specs=[pl.BlockSpec((tm, d), lambda i: (i, 0)),
                  pl.BlockSpec((1, d), lambda i: (0, 0))],
        out_specs=pl.BlockSpec((tm, d), lambda i: (i, 0)),
        compiler_params=_cparams(("parallel",), 32),
        name="rmsnorm",
    )(x2, g.reshape(1, d))


def _inproj_conv_kernel(x_ref, g_ref, w_ref, wdt_ref, cw_ref, cb_ref, o_ref, dt_ref, hn_sc, hbuf_sc, halo_sc,
                        *, tm, tn, n_conv, steps_per_batch):
    taps = cw_ref.shape[0]

    @pl.when(pl.program_id(0) % steps_per_batch == 0)
    def _():
        halo_sc[...] = jnp.zeros_like(halo_sc)

    hn_sc[...] = _rms_rows(x_ref[...], g_ref[...]).astype(BF16)
    dt_ref[...] = _dot(hn_sc[...], wdt_ref[...])
    for j in range(w_ref.shape[1] // tn):
        cols = slice(j * tn, (j + 1) * tn)
        y = _dot(hn_sc[...], w_ref[:, cols])
        if j * tn < n_conv:
            buf = hbuf_sc.at[j % 2]
            buf[0:SUBLANES, :] = halo_sc[j]
            buf[SUBLANES:SUBLANES + tm, :] = y
            halo_sc[j] = buf[tm:tm + SUBLANES, :]
            y = cb_ref[:, cols]
            for k in range(taps):
                off = SUBLANES - (taps - 1) + k
                y = y + cw_ref[k:k + 1, cols] * buf[off:off + tm, :]
            y = _silu(y)
        o_ref[:, cols] = y.astype(o_ref.dtype)


def _inproj_conv(x2, g, w, w_dt, conv_w, conv_b, seq, tn, tm):
    t, d = x2.shape
    n = w.shape[1]
    ndt = w_dt.shape[1]
    n_conv = conv_w.shape[1]
    tm = min(tm, seq)
    kern = functools.partial(_inproj_conv_kernel, tm=tm, tn=tn, n_conv=n_conv, steps_per_batch=seq // tm)
    return pl.pallas_call(
        kern,
        out_shape=(jax.ShapeDtypeStruct((t, n), BF16), jax.ShapeDtypeStruct((t, ndt), F32)),
        grid=(t // tm,),
        in_specs=[pl.BlockSpec((tm, d), lambda i: (i, 0)),
                  _resident((1, d), lambda i: (0, 0)),
                  _resident((d, n), lambda i: (0, 0)),
                  _resident((d, ndt), lambda i: (0, 0)),
                  _resident(conv_w.shape, lambda i: (0, 0)),
                  _resident((1, n_conv), lambda i: (0, 0))],
        out_specs=(pl.BlockSpec((tm, n), lambda i: (i, 0)), pl.BlockSpec((tm, ndt), lambda i: (i, 0))),
        scratch_shapes=[pltpu.VMEM((tm, d), BF16),
                        pltpu.VMEM((2, tm + SUBLANES, tn), F32),
                        pltpu.VMEM((n_conv // tn, SUBLANES, tn), F32)],
        compiler_params=_cparams(("arbitrary",), 48),
        name="inproj_conv",
    )(x2, g.reshape(1, d), w, w_dt, conv_w, conv_b.reshape(1, -1))


def _ffn_kernel(r_ref, a_ref, wm_ref, g_ref, wup_ref, cw_ref, cb_ref, wdn_ref, gf_ref, o_ref,
                x_sc, hn_sc, hbuf_sc, halo_sc, act_sc, *, tm, ffn, steps_per_batch, glu, final_norm):
    cols = FFN_COLS
    nchunk = ffn // cols

    @pl.when(pl.program_id(0) % steps_per_batch == 0)
    def _():
        halo_sc[...] = jnp.zeros_like(halo_sc)

    a = a_ref[...].astype(BF16)
    if glu:
        n = r_ref.shape[1]
        x_sc[...] = r_ref[...] + _dot(a, wm_ref[:, :n]) * _sigmoid(_dot(a, wm_ref[:, n:]))
    else:
        x_sc[...] = r_ref[...] + _dot(a, wm_ref[...])
    hn_sc[...] = _rms_rows(x_sc[...], g_ref[...]).astype(BF16)

    def up(c):
        lo = c * cols
        buf = hbuf_sc.at[c % 2]
        hn = hn_sc[...]
        buf[0:SUBLANES, :] = halo_sc[c]
        buf[SUBLANES:SUBLANES + tm, 0:cols] = _dot(hn, wup_ref[:, lo:lo + cols])
        buf[SUBLANES:SUBLANES + tm, cols:2 * cols] = _dot(hn, wup_ref[:, ffn + lo:ffn + lo + cols])
        halo_sc[c] = buf[tm:tm + SUBLANES, :]

    def activate(c):
        lo = c * cols
        buf = hbuf_sc.at[c % 2]
        w3 = jnp.concatenate([cw_ref[:, lo:lo + cols], cw_ref[:, ffn + lo:ffn + lo + cols]], axis=1)
        y = jnp.concatenate([cb_ref[:, lo:lo + cols], cb_ref[:, ffn + lo:ffn + lo + cols]], axis=1)
        for k in range(FFN_CONV):
            off = SUBLANES - (FFN_CONV - 1) + k
            y = y + w3[k:k + 1, :] * buf[off:off + tm, :]
        act_sc[:, lo:lo + cols] = (_silu(y[:, :cols]) * y[:, cols:]).astype(BF16)

    up(0)
    for c in range(nchunk):
        if c + 1 < nchunk:
            up(c + 1)
        activate(c)
    y = x_sc[...] + _dot(act_sc[...], wdn_ref[...])
    o_ref[...] = _rms_rows(y, gf_ref[...]) if final_norm else y


def _mixer_out_ffn(res, a, wm, glu, g, w_up, conv_w, conv_b, w_down, gf, final_norm, seq, rows):
    t, d = res.shape
    ka = a.shape[1]
    ffn = w_down.shape[0]
    tm = min(rows, seq)
    kern = functools.partial(_ffn_kernel, tm=tm, ffn=ffn, steps_per_batch=seq // tm,
                             glu=glu, final_norm=final_norm)
    return pl.pallas_call(
        kern,
        out_shape=jax.ShapeDtypeStruct((t, d), F32),
        grid=(t // tm,),
        in_specs=[pl.BlockSpec((tm, d), lambda i: (i, 0)),
                  pl.BlockSpec((tm, ka), lambda i: (i, 0)),
                  _resident(wm.shape, lambda i: (0, 0)),
                  _resident((1, d), lambda i: (0, 0)),
                  _resident((d, 2 * ffn), lambda i: (0, 0)),
                  _resident((FFN_CONV, 2 * ffn), lambda i: (0, 0)),
                  _resident((1, 2 * ffn), lambda i: (0, 0)),
                  _resident((ffn, d), lambda i: (0, 0)),
                  _resident((1, d), lambda i: (0, 0))],
        out_specs=pl.BlockSpec((tm, d), lambda i: (i, 0)),
        scratch_shapes=[pltpu.VMEM((tm, d), F32),
                        pltpu.VMEM((tm, d), BF16),
                        pltpu.VMEM((2, tm + SUBLANES, 2 * FFN_COLS), F32),
                        pltpu.VMEM((ffn // FFN_COLS, SUBLANES, 2 * FFN_COLS), F32),
                        pltpu.VMEM((tm, ffn), BF16)],
        compiler_params=_cparams(("arbitrary",), 52),
        name="mixer_out_ffn",
    )(res, a, wm.astype(BF16), g.reshape(1, d), w_up.astype(BF16), conv_w, conv_b.reshape(1, -1),
      w_down.astype(BF16), gf.reshape(1, d))


def _qkv_kernel(x_ref, g_ref, w_ref, wr_ref, cos_ref, sin_ref, o_ref, hn_sc, *, tn):
    hn_sc[...] = _rms_rows(x_ref[...], g_ref[...]).astype(BF16)
    reps = tn // LANES
    n_rope = wr_ref.shape[1]
    for j in range(w_ref.shape[1] // tn):
        cols = slice(j * tn, (j + 1) * tn)
        y = _dot(hn_sc[...], w_ref[:, cols])
        if j * tn < n_rope:
            cos = jnp.concatenate([cos_ref[...]] * reps, axis=1)
            sin = jnp.concatenate([sin_ref[...]] * reps, axis=1)
            y = y * cos + _dot(hn_sc[...], wr_ref[:, cols]) * sin
        o_ref[:, cols] = y.astype(o_ref.dtype)


def _rope_tables(seq):
    half = ROPE_DIM // 2
    pos = jnp.arange(seq, dtype=F32)
    inv_freq = ROPE_THETA ** (-jnp.arange(0, ROPE_DIM, 2, dtype=F32) / ROPE_DIM)
    ang = pos[:, None] * inv_freq[None, :]
    cos, sin = jnp.cos(ang), jnp.sin(ang)
    lane = jnp.arange(LANES) % DA_HEAD_DIM
    idx = lane % half
    cos_l = jnp.where(lane[None, :] < ROPE_DIM, cos[:, idx], 1.0)
    sin_l = jnp.where(lane[None, :] < ROPE_DIM, sin[:, idx], 0.0)
    return cos_l, sin_l


def _rope_partner_weight(w):
    half = ROPE_DIM // 2
    col = jnp.arange(w.shape[1])
    lane = col % DA_HEAD_DIM
    partner = jnp.where(lane < half, col + half, col - half)
    sign = jnp.where(lane < half, -1.0, jnp.where(lane < ROPE_DIM, 1.0, 0.0))
    return w[:, jnp.clip(partner, 0, w.shape[1] - 1)] * sign[None, :]


def _qkv_proj(x2, g, w_qkv, seq):
    t, d = x2.shape
    width = w_qkv.shape[1] // 3
    scale = DA_HEAD_DIM ** -0.5 * math.log2(math.e)
    w = jnp.concatenate([w_qkv[:, :width] * scale, w_qkv[:, width:]], axis=1).astype(BF16)
    wr = _rope_partner_weight(w[:, :2 * width])
    n = w.shape[1]
    tm = min(512, seq)
    tn = 512
    cos_l, sin_l = _rope_tables(seq)
    spb = seq // tm
    tab = pl.BlockSpec((tm, LANES), lambda i: (i % spb, 0))
    return pl.pallas_call(
        functools.partial(_qkv_kernel, tn=tn),
        out_shape=jax.ShapeDtypeStruct((t, n), BF16),
        grid=(t // tm,),
        in_specs=[pl.BlockSpec((tm, d), lambda i: (i, 0)),
                  _resident((1, d), lambda i: (0, 0)),
                  _resident((d, n), lambda i: (0, 0)),
                  _resident((d, 2 * width), lambda i: (0, 0)),
                  tab, tab],
        out_specs=pl.BlockSpec((tm, n), lambda i: (i, 0)),
        scratch_shapes=[pltpu.VMEM((tm, d), BF16)],
        compiler_params=_cparams(("parallel",), 40),
        name="qkv_rope",
    )(x2, g.reshape(1, d), w, wr, cos_l, sin_l)


def _attn_kernel(q_ref, k_ref, v_ref, lq1_ref, lk1_ref, lq2_ref, lk2_ref, gain_ref, o_ref,
                 vt_sc, qt_sc, s_sc, mx_sc, acc_sc, m_sc, l_sc, *, tq, seq, lambda_init, unroll):
    qi = pl.program_id(2)
    nkv = seq // tq

    @pl.when(qi == 0)
    def _():
        def tr(t, carry):
            st = pl.multiple_of(t * tq, tq)
            vt_sc[t] = v_ref[pl.ds(st, tq), :].astype(F32).T.astype(BF16)
            return carry
        lax.fori_loop(0, nkv, tr, 0)

    qt = q_ref[...].astype(F32).T
    row = lax.broadcasted_iota(jnp.int32, qt.shape, 0)
    qt_sc[0] = jnp.where(row < DA_HEAD_DIM, qt, 0.0).astype(BF16)
    qt_sc[1] = jnp.where(row >= DA_HEAD_DIM, qt, 0.0).astype(BF16)

    m_sc[...] = jnp.full_like(m_sc, -jnp.inf)
    l_sc[...] = jnp.zeros_like(l_sc)
    acc_sc[...] = jnp.zeros_like(acc_sc)

    def scores(t, slot, masked):
        st = pl.multiple_of(t * tq, tq)
        kt = k_ref[pl.ds(st, tq), :]
        if masked:
            kc = _idiv(lax.broadcasted_iota(jnp.int32, (tq, tq), 0), CHUNK)
            qc = _idiv(lax.broadcasted_iota(jnp.int32, (tq, tq), 1), CHUNK)
            keep = kc <= qc
        for c in range(2):
            s = _dot(kt, qt_sc[c])
            if masked:
                s = jnp.where(keep, s, -jnp.inf)
            s_sc[slot, c] = s
            mx_sc[slot, c] = jnp.max(s, axis=0, keepdims=True)

    def accumulate(t, slot):
        vt = vt_sc[t]
        for c in range(2):
            m_prev = m_sc[c]
            m_new = jnp.maximum(m_prev, mx_sc[slot, c])
            alpha = jnp.exp2(m_prev - m_new)
            p = jnp.exp2(s_sc[slot, c] - m_new)
            l_sc[c] = alpha * l_sc[c] + jnp.sum(p, axis=0, keepdims=True)
            acc_sc[c] = alpha * acc_sc[c] + _dot(vt, p.astype(BF16))
            m_sc[c] = m_new

    scores(qi, 0, True)

    def run(width):
        def body(u, carry):
            for j in range(width):
                t = width * u + j
                scores(t, (j + 1) % 2, False)
                accumulate(jnp.where(t == 0, qi, t - 1), j % 2)
            return carry
        return body

    nmain = qi // unroll
    npair = qi // 2
    lax.fori_loop(0, nmain, run(unroll), 0)
    lax.fori_loop(nmain * (unroll // 2), npair, run(2), 0)
    odd = qi % 2 == 1

    @pl.when(odd)
    def _():
        scores(qi - 1, 1, False)
        accumulate(jnp.where(qi == 1, qi, qi - 2), 0)
        accumulate(qi - 1, 1)

    @pl.when(jnp.logical_not(odd))
    def _():
        accumulate(jnp.where(qi == 0, qi, qi - 1), 0)

    lam =(jnp.exp(jnp.sum(lq1_ref[...] * lk1_ref[...], axis=-1, keepdims=True))
           - jnp.exp(jnp.sum(lq2_ref[...] * lk2_ref[...], axis=-1, keepdims=True)) + lambda_init)
    o = acc_sc[0] / l_sc[0] - lam * (acc_sc[1] / l_sc[1])
    ms = jnp.mean(o * o, axis=0, keepdims=True)
    on = (o * lax.rsqrt(ms + EPS)).T
    o_ref[...] = (on * gain_ref[...] * (1.0 - lambda_init)).astype(o_ref.dtype)


def _diff_attention_core(qkv, lq1, lk1, lq2, lk2, sub_gain, lambda_init, batch, seq, unroll):
    width = DA_HEADS * DA_VALUE_DIM
    qkv3 = qkv.reshape(batch, seq, 3 * width)
    tq = min(512, seq)
    kern = functools.partial(_attn_kernel, tq=tq, seq=seq, lambda_init=lambda_init, unroll=unroll)
    vec = pl.BlockSpec((1, DA_HEAD_DIM), lambda b, h, i: (0, 0))
    return pl.pallas_call(
        kern,
        out_shape=jax.ShapeDtypeStruct((batch, seq, width), BF16),
        grid=(batch, DA_HEADS, seq // tq),
        in_specs=[pl.BlockSpec((None, tq, DA_VALUE_DIM), lambda b, h, i: (b, i, h)),
                  pl.BlockSpec((None, seq, DA_VALUE_DIM), lambda b, h, i: (b, 0, DA_HEADS + h)),
                  pl.BlockSpec((None, seq, DA_VALUE_DIM), lambda b, h, i: (b, 0, 2 * DA_HEADS + h)),
                  vec, vec, vec, vec,
                  pl.BlockSpec((1, DA_VALUE_DIM), lambda b, h, i: (0, 0))],
        out_specs=pl.BlockSpec((None, tq, DA_VALUE_DIM), lambda b, h, i: (b, i, h)),
        scratch_shapes=[pltpu.VMEM((seq // tq, DA_VALUE_DIM, tq), BF16),
                        pltpu.VMEM((2, DA_VALUE_DIM, tq), BF16),
                        pltpu.VMEM((2, 2, tq, tq), F32),
                        pltpu.VMEM((2, 2, 1, tq), F32),
                        pltpu.VMEM((2, DA_VALUE_DIM, tq), F32),
                        pltpu.VMEM((2, 1, tq), F32),
                        pltpu.VMEM((2, 1, tq), F32)],
        compiler_params=_cparams(("parallel", "parallel", "arbitrary"), 48),
        name="diff_attention",
    )(qkv3, qkv3, qkv3, lq1.reshape(1, -1), lk1.reshape(1, -1), lq2.reshape(1, -1),
      lk2.reshape(1, -1), sub_gain.reshape(1, -1)).reshape(batch * seq, width)


def _ssd_kernel(xbc_ref, z_ref, dt_ref, dtb_ref, alog_ref, dsk_ref, ng_ref,
                tril_ref, e128_ref, e64_ref, y_ref, state_sc, *, steps_per_batch, inner):
    L = SSD_CHUNK
    G = SSM_GROUPS
    N = SSM_STATE
    gw = inner // G
    R = gw // SSM_HEAD_DIM

    @pl.when(pl.program_id(0) % steps_per_batch == 0)
    def _():
        state_sc[...] = jnp.zeros_like(state_sc)

    xbc = xbc_ref[...].astype(F32)

    x = dt_ref[...] + dtb_ref[...]
    dt = jnp.maximum(x, 0.0) + jnp.log(1.0 + jnp.exp(-jnp.abs(x)))
    adt = dt * (-jnp.exp(alog_ref[...]))

    acum = _dot3_lhs01(tril_ref[...], adt)
    acol128 = _dot3_rhs01(acum, e128_ref[...])
    acol64 = _dot3_rhs01(acum, e64_ref[...])
    dt64 = _dot3_rhs01(dt, e64_ref[...])

    ri = lax.broadcasted_iota(jnp.int32, acol128.shape, 0)
    ci = lax.broadcasted_iota(jnp.int32, acol128.shape, 1) & (L - 1)
    arow = jnp.sum(jnp.where(ri == ci, acol128, 0.0), axis=0, keepdims=True)
    decay_in = jnp.exp(jnp.where(ci <= ri, acol128 - arow, -jnp.inf))

    xs = xbc[:, :inner]
    xdt = xs * dt64
    last = acol64[L - 1:L, :]
    xdo = (xdt * jnp.exp(last - acol64)).astype(BF16)
    xdt = xdt.astype(BF16)
    e_a = jnp.exp(acol64)
    e_last = jnp.exp(last)

    rr = _idiv(lax.broadcasted_iota(jnp.int32, (R * L, gw), 0), L)
    cc = _idiv(lax.broadcasted_iota(jnp.int32, (R * L, gw), 1), SSM_HEAD_DIM)
    blockdiag = rr == cc

    for g in range(G):
        bg = xbc[:, inner + g * N: inner + (g + 1) * N]
        cg = xbc[:, inner + G * N + g * N: inner + G * N + (g + 1) * N].astype(BF16)
        bgt = bg.T.astype(BF16)
        cb = _dot(cg, bgt)
        cbt = jnp.concatenate([cb] * R, axis=1)
        mg = (cbt * decay_in[:, g * R * L:(g + 1) * R * L]).astype(BF16)
        xg = xdt[:, g * gw:(g + 1) * gw]
        xbd = jnp.where(blockdiag, jnp.concatenate([xg] * R, axis=0), jnp.zeros((), BF16))
        y = _dot(mg, xbd)
        st = state_sc[g]
        y = y + _dot(cg, st.astype(BF16)) * e_a[:, g * gw:(g + 1) * gw]
        state_sc[g] = st * e_last[:, g * gw:(g + 1) * gw] + _dot(bgt, xdo[:, g * gw:(g + 1) * gw])
        y = y + dsk_ref[:, g * gw:(g + 1) * gw] * xs[:, g * gw:(g + 1) * gw]
        y = y * _silu(z_ref[:, g * gw:(g + 1) * gw].astype(F32))
        ms = jnp.mean(y * y, axis=-1, keepdims=True)
        y_ref[:, g * gw:(g + 1) * gw] = (y * lax.rsqrt(ms + EPS) * ng_ref[:, g * gw:(g + 1) * gw]).astype(y_ref.dtype)


def _mamba2_mixer(x2, g, w_in, conv_w, conv_b, dt_bias, a_log, d_skip, norm_gain, seq):
    t, d = x2.shape
    nh = dt_bias.shape[0]
    inner = nh * SSM_HEAD_DIM
    conv_dim = conv_w.shape[1]
    L = SSD_CHUNK
    w_xz = jnp.concatenate([w_in[:, inner:inner + conv_dim], w_in[:, :inner]], axis=1).astype(BF16)
    w_dt = jnp.pad(w_in[:, inner + conv_dim:], ((0, 0), (0, LANES - nh))).astype(BF16)
    xz, dtr = _inproj_conv(x2, g, w_xz, w_dt, conv_w, conv_b, seq, 1024, 512)

    pad1 = lambda v: jnp.pad(v, (0, LANES - nh)).reshape(1, LANES)
    rows = jnp.arange(L)
    tril = (rows[None, :] <= rows[:, None]).astype(BF16)
    head = jnp.arange(LANES)
    e128 = (head[:, None] == (jnp.arange(nh * L) // L)[None, :]).astype(BF16)
    e64 = (head[:, None] == (jnp.arange(inner) // SSM_HEAD_DIM)[None, :]).astype(BF16)
    dsk = jnp.repeat(d_skip, SSM_HEAD_DIM).reshape(1, inner)

    nblk_x = conv_dim // inner
    kern = functools.partial(_ssd_kernel, steps_per_batch=seq // L, inner=inner)
    const = lambda shape: _resident(shape, lambda i: (0, 0))
    y = pl.pallas_call(
        kern,
        out_shape=jax.ShapeDtypeStruct((t, inner), BF16),
        grid=(t // L,),
        in_specs=[pl.BlockSpec((L, conv_dim), lambda i: (i, 0)),
                  pl.BlockSpec((L, inner), lambda i: (i, nblk_x)),
                  pl.BlockSpec((L, LANES), lambda i: (i, 0)),
                  const((1, LANES)), const((1, LANES)),
                  const((1, inner)), const((1, inner)),
                  const((L, L)), const((LANES, nh * L)), const((LANES, inner))],
        out_specs=pl.BlockSpec((L, inner), lambda i: (i, 0)),
        scratch_shapes=[pltpu.VMEM((SSM_GROUPS, SSM_STATE, inner // SSM_GROUPS), F32)],
        compiler_params=_cparams(("arbitrary",), 48),
        name="ssd_scan",
    )(xz, xz, dtr, pad1(dt_bias), pad1(a_log), dsk,
      norm_gain.reshape(1, -1), tril, e128, e64)
    return y


def _s5_tables(a_re, a_im, log_dt, b_re, b_im, c_re, c_im, d_skip):
    L = S5_CHUNK
    G, P = a_re.shape
    K = b_re.shape[-1]
    lam = lax.complex(jnp.minimum(a_re, -1e-4), a_im)
    delta = jnp.exp(log_dt)[:, None]
    lam_bar = jnp.exp(lam * delta)
    b_bar = ((lam_bar - 1.0) / lam)[..., None] * lax.complex(b_re, b_im)
    c = lax.complex(c_re, c_im)
    tau = jnp.arange(L + 1, dtype=F32)
    pw = jnp.exp(tau[:, None, None] * (lam * delta)[None])

    gs = LANES // K
    ns = G // gs
    eye = jnp.eye(gs, dtype=F32)
    kt = jnp.einsum("gkp,tgp,gpi->gtki", c, pw[:L], b_bar).real
    lag = jnp.arange(L)[None, :] - jnp.arange(L)[:, None]
    blk = jnp.einsum("sgtoi,gh->stgiho", kt.reshape(ns, gs, L, K, K), eye).reshape(ns, L, LANES, LANES)
    m = jnp.where((lag >= 0)[None, :, :, None, None], blk[:, jnp.maximum(lag, 0)], 0.0)
    m = m.transpose(0, 1, 3, 2, 4).reshape(ns, L * LANES, L * LANES)

    bst = pw[:L][::-1].transpose(1, 0, 2)[:, :, None, :] * b_bar.transpose(0, 2, 1)[:, None, :, :]
    bst = bst.reshape(ns, gs, L, K, P)
    put_b = lambda v: jnp.einsum("sgjip,gh->sjgihp", v, eye).reshape(ns, L * LANES, gs * P)
    w = c.transpose(0, 2, 1)[:, :, None, :] * pw[1:].transpose(1, 2, 0)[:, :, :, None]
    w = w.reshape(ns, gs, P, L, K)
    put_c = lambda v: jnp.einsum("sgpto,gh->sgptho", v, eye).reshape(ns, gs * P, L * LANES)
    lam_l = pw[L].reshape(ns, 1, gs * P)
    return dict(
        m=m.astype(BF16),
        bst_re=put_b(bst.real).astype(BF16), bst_im=put_b(bst.imag).astype(BF16),
        cst_re=put_c(w.real).astype(BF16), cst_im=put_c(-w.imag).astype(BF16),
        lam_re=lam_l.real, lam_im=lam_l.imag, dsk=d_skip.reshape(ns, 1, LANES))


def _s5_kernel(u_ref, m_ref, bre_ref, bim_ref, cre_ref, cim_ref, lre_ref, lim_ref, dsk_ref, o_ref,
               a_sc, s_re_sc, s_im_sc, in_re_sc, in_im_sc, carry_sc, *, rows, batch):
    L = S5_CHUNK

    @pl.when(pl.program_id(1) == 0)
    def _():
        carry_sc[...] = jnp.zeros_like(carry_sc)

    for b in range(batch):
        for j in range(L):
            a_sc[b * rows:(b + 1) * rows, j * LANES:(j + 1) * LANES] = (
                u_ref[b, pl.ds(j, rows, stride=L), :].astype(BF16))

    s_re_sc[...] = _dot(a_sc[...], bre_ref[0])
    s_im_sc[...] = _dot(a_sc[...], bim_ref[0])
    lre = lre_ref[0]
    lim = lim_ref[0]

    def body(c, carry):
        out = []
        for b in range(batch):
            sre, sim = carry[b]
            r = b * rows + c
            in_re_sc[pl.ds(r, 1), :] = sre
            in_im_sc[pl.ds(r, 1), :] = sim
            out.append((lre * sre - lim * sim + s_re_sc[pl.ds(r, 1), :],
                        lre * sim + lim * sre + s_im_sc[pl.ds(r, 1), :]))
        return tuple(out)

    init = tuple((carry_sc[2 * b:2 * b + 1, :], carry_sc[2 * b + 1:2 * b + 2, :]) for b in range(batch))
    last = lax.fori_loop(0, rows, body, init)
    for b in range(batch):
        carry_sc[2 * b:2 * b + 1, :] = last[b][0]
        carry_sc[2 * b + 1:2 * b + 2, :] = last[b][1]

    sre = in_re_sc[...].astype(BF16)
    sim = in_im_sc[...].astype(BF16)
    pairw = 2 * LANES
    for p in range(L // 2):
        cols = slice(p * pairw, (p + 1) * pairw)
        kk = (p + 1) * pairw
        y2 = (_dot(a_sc[:, :kk], m_ref[0, :kk, cols])
              + _dot(sre, cre_ref[0, :, cols]) + _dot(sim, cim_ref[0, :, cols]))
        for h in range(2):
            t = 2 * p + h
            for b in range(batch):
                y = (y2[b * rows:(b + 1) * rows, h * LANES:(h + 1) * LANES]
                     + dsk_ref[0] * u_ref[b, pl.ds(t, rows, stride=L), :])
                inner = math.sqrt(2.0 / math.pi) * (y + 0.044715 * (y * y * y))
                o_ref[b, pl.ds(t, rows, stride=L), :] = 0.5 * y * (1.0 + jnp.tanh(inner))


def _s5_mixer(x2, g, a_re, a_im, log_dt, b_re, b_im, c_re, c_im, d_skip, seq):
    t, d = x2.shape
    L = S5_CHUNK
    G, K = d_skip.shape
    tab = _s5_tables(a_re, a_im, log_dt, b_re, b_im, c_re, c_im, d_skip)
    u = _rmsnorm(x2, g, F32)
    ns = d // LANES
    sw = (LANES // K) * S5_STATE
    batch = t // seq
    assert 2 * batch <= SUBLANES
    steps = min(S5_STEPS, seq)
    rows = steps // L
    lw = L * LANES
    kern = functools.partial(_s5_kernel, rows=rows, batch=batch)
    slab = lambda shape: pl.BlockSpec(shape, lambda s, i: (s, 0, 0))
    ya = pl.pallas_call(
        kern,
        out_shape=jax.ShapeDtypeStruct((batch, seq, d), F32),
        grid=(ns, seq // steps),
        in_specs=[pl.BlockSpec((batch, steps, LANES), lambda s, i: (0, i, s)),
                  slab((1, lw, lw)), slab((1, lw, sw)), slab((1, lw, sw)),
                  slab((1, sw, lw)), slab((1, sw, lw)),
                  slab((1, 1, sw)), slab((1, 1, sw)), slab((1, 1, LANES))],
        out_specs=pl.BlockSpec((batch, steps, LANES), lambda s, i: (0, i, s)),
        scratch_shapes=[pltpu.VMEM((batch * rows, lw), BF16)] + [pltpu.VMEM((batch * rows, sw), F32)] * 4
                       + [pltpu.VMEM((SUBLANES, sw), F32)],
        compiler_params=_cparams(("parallel", "arbitrary"), 48),
        name="s5_scan",
    )(u.reshape(batch, seq, d), tab["m"], tab["bst_re"], tab["bst_im"], tab["cst_re"], tab["cst_im"],
      tab["lam_re"], tab["lam_im"], tab["dsk"]).reshape(t, d)
    return ya


def kernel(x, norm_mix, norm_ffn, norm_final, attn_w_qkv, attn_w_o, attn_lambda_q1, attn_lambda_k1, attn_lambda_q2, attn_lambda_k2, attn_sub_gain, ssm_w_in, ssm_conv_w, ssm_conv_b, ssm_dt_bias, ssm_a_log, ssm_d, ssm_norm_gain, ssm_w_out, s5_a_re, s5_a_im, s5_log_dt, s5_b_re, s5_b_im, s5_c_re, s5_c_im, s5_d, s5_w_glu, ffn_w_up, ffn_conv_w, ffn_conv_b, ffn_w_down):
    batch, seq, d = x.shape
    depth = norm_mix.shape[0]
    x2 = x.reshape(batch * seq, d)
    for i in range(depth):
        kind = i % N_MIXERS
        j = i // N_MIXERS
        if kind == 0:
            lambda_init = 0.8 - 0.6 * math.exp(-0.3 * i)
            qkv = _qkv_proj(x2, norm_mix[i], attn_w_qkv[j], seq)
            a = _diff_attention_core(qkv, attn_lambda_q1[j], attn_lambda_k1[j], attn_lambda_q2[j],
                                     attn_lambda_k2[j], attn_sub_gain[j], lambda_init, batch, seq,
                                     ATTN_UNROLL[j])
            wm = attn_w_o[j]
        elif kind == 1:
            a = _mamba2_mixer(x2, norm_mix[i], ssm_w_in[j], ssm_conv_w[j], ssm_conv_b[j], ssm_dt_bias[j],
                              ssm_a_log[j], ssm_d[j], ssm_norm_gain[j], seq)
            wm = ssm_w_out[j]
        else:
            a = _s5_mixer(x2, norm_mix[i], s5_a_re[j], s5_a_im[j], s5_log_dt[j], s5_b_re[j], s5_b_im[j],
                          s5_c_re[j], s5_c_im[j], s5_d[j], seq)
            wm = s5_w_glu[j]
        x2 = _mixer_out_ffn(x2, a, wm, kind == 2, norm_ffn[i], ffn_w_up[i], ffn_conv_w[i], ffn_conv_b[i],
                            ffn_w_down[i], norm_final, i == depth - 1, seq, FFN_ROWS[i])
    return x2.reshape(batch, seq, d)
```

```python
import functools
import math

import jax
import jax.numpy as jnp
from jax import lax
from jax.experimental import pallas as pl
from jax.experimental.pallas import tpu as pltpu

F32 = jnp.float32
BF16 = jnp.bfloat16

EPS = 1e-6
N_MIXERS = 3
CHUNK = 64
DA_HEADS = 8
DA_HEAD_DIM = 64
DA_VALUE_DIM = 2 * DA_HEAD_DIM
ROPE_THETA = 500000.0
ROPE_DIM = DA_HEAD_DIM // 4
SSM_HEAD_DIM = 64
SSM_GROUPS = 8
SSM_STATE = 128
SSM_CONV = 4
SSD_CHUNK = 128
S5_GROUP = 16
S5_STATE = 64
S5_CHUNK = 8
S5_STEPS = 2048
ATTN_UNROLL = 4
ATTN_PAIR = (False, True)
FFN_CONV = 3
FFN_COLS = 256
FFN_ROWS = 512
LANES = 128
SUBLANES = 8
MIB = 1024 * 1024


def _cparams(sem, vmem_mib):
    return pltpu.CompilerParams(dimension_semantics=sem, vmem_limit_bytes=vmem_mib * MIB)


def _resident(shape, index_map):
    return pl.BlockSpec(shape, index_map, pipeline_mode=pl.Buffered(1))


def _rms_rows(x, g):
    ms = jnp.mean(x * x, axis=-1, keepdims=True)
    return x * lax.rsqrt(ms + EPS) * g


def _sigmoid(x):
    return 0.5 + 0.5 * jnp.tanh(0.5 * x)


def _silu(x):
    h = 0.5 * x
    return h + h * jnp.tanh(h)


def _idiv(v, n):
    return lax.shift_right_logical(v, int(math.log2(n)))


def _split3(x):
    hi = x.astype(BF16)
    r = x - hi.astype(F32)
    mid = r.astype(BF16)
    lo = (r - mid.astype(F32)).astype(BF16)
    return hi, mid, lo


def _dot(a, b):
    return jnp.dot(a, b, preferred_element_type=F32)


def _dot2_rhs01(x, e):
    hi = x.astype(BF16)
    mid = (x - hi.astype(F32)).astype(BF16)
    return _dot(hi, e) + _dot(mid, e)


def _dot3_lhs01(e, x):
    hi, mid, lo = _split3(x)
    return _dot(e, hi) + _dot(e, mid) + _dot(e, lo)


def _norm_kernel(x_ref, g_ref, o_ref):
    o_ref[...] = _rms_rows(x_ref[...], g_ref[...]).astype(o_ref.dtype)


def _rmsnorm(x2, g, out_dtype):
    t, d = x2.shape
    tm = min(512, t)
    return pl.pallas_call(
        _norm_kernel,
        out_shape=jax.ShapeDtypeStruct((t, d), out_dtype),
        grid=(t // tm,),
        in_specs=[pl.BlockSpec((tm, d), lambda i: (i, 0)),
                  pl.BlockSpec((1, d), lambda i: (0, 0))],
        out_specs=pl.BlockSpec((tm, d), lambda i: (i, 0)),
        compiler_params=_cparams(("parallel",), 32),
        name="rmsnorm",
    )(x2, g.reshape(1, d))


def _inproj_conv_kernel(x_ref, g_ref, w_ref, wdt_ref, cw_ref, cb_ref, o_ref, dt_ref, hn_sc, hbuf_sc, halo_sc,
                        *, tm, tn, n_conv, steps_per_batch):
    taps = cw_ref.shape[0]

    @pl.when(pl.program_id(0) % steps_per_batch == 0)
    def _():
        halo_sc[...] = jnp.zeros_like(halo_sc)

    hn_sc[...] = _rms_rows(x_ref[...], g_ref[...]).astype(BF16)
    dt_ref[...] = _dot(hn_sc[...], wdt_ref[...])
    for j in range(w_ref.shape[1] // tn):
        cols = slice(j * tn, (j + 1) * tn)
        y = _dot(hn_sc[...], w_ref[:, cols])
        if j * tn < n_conv:
            buf = hbuf_sc.at[j % 2]
            buf[0:SUBLANES, :] = halo_sc[j]
            buf[SUBLANES:SUBLANES + tm, :] = y
            halo_sc[j] = buf[tm:tm + SUBLANES, :]
            y = cb_ref[:, cols]
            for k in range(taps):
                off = SUBLANES - (taps - 1) + k
                y = y + cw_ref[k:k + 1, cols] * buf[off:off + tm, :]
            y = _silu(y)
        o_ref[:, cols] = y.astype(o_ref.dtype)


def _inproj_conv(x2, g, w, w_dt, conv_w, conv_b, seq, tn, tm):
    t, d = x2.shape
    n = w.shape[1]
    ndt = w_dt.shape[1]
    n_conv = conv_w.shape[1]
    tm = min(tm, seq)
    kern = functools.partial(_inproj_conv_kernel, tm=tm, tn=tn, n_conv=n_conv, steps_per_batch=seq // tm)
    return pl.pallas_call(
        kern,
        out_shape=(jax.ShapeDtypeStruct((t, n), BF16), jax.ShapeDtypeStruct((t, ndt), F32)),
        grid=(t // tm,),
        in_specs=[pl.BlockSpec((tm, d), lambda i: (i, 0)),
                  _resident((1, d), lambda i: (0, 0)),
                  _resident((d, n), lambda i: (0, 0)),
                  _resident((d, ndt), lambda i: (0, 0)),
                  _resident(conv_w.shape, lambda i: (0, 0)),
                  _resident((1, n_conv), lambda i: (0, 0))],
        out_specs=(pl.BlockSpec((tm, n), lambda i: (i, 0)), pl.BlockSpec((tm, ndt), lambda i: (i, 0))),
        scratch_shapes=[pltpu.VMEM((tm, d), BF16),
                        pltpu.VMEM((2, tm + SUBLANES, tn), F32),
                        pltpu.VMEM((n_conv // tn, SUBLANES, tn), F32)],
        compiler_params=_cparams(("arbitrary",), 48),
        name="inproj_conv",
    )(x2, g.reshape(1, d), w, w_dt, conv_w, conv_b.reshape(1, -1))


def _ffn_kernel(r_ref, a_ref, wm_ref, g_ref, wup_ref, cw_ref, cb_ref, wdn_ref, gf_ref, o_ref,
                x_sc, hn_sc, hbuf_sc, halo_sc, act_sc, *, tm, ffn, steps_per_batch, glu, final_norm):
    cols = FFN_COLS
    nchunk = ffn // cols

    @pl.when(pl.program_id(0) % steps_per_batch == 0)
    def _():
        halo_sc[...] = jnp.zeros_like(halo_sc)

    a = a_ref[...].astype(BF16)
    if glu:
        n = r_ref.shape[1]
        x_sc[...] = r_ref[...] + _dot(a, wm_ref[:, :n]) * _sigmoid(_dot(a, wm_ref[:, n:]))
    else:
        x_sc[...] = r_ref[...] + _dot(a, wm_ref[...])
    hn_sc[...] = _rms_rows(x_sc[...], g_ref[...]).astype(BF16)

    def up(c):
        lo = c * cols
        buf = hbuf_sc.at[c % 2]
        hn = hn_sc[...]
        buf[0:SUBLANES, :] = halo_sc[c]
        buf[SUBLANES:SUBLANES + tm, 0:cols] = _dot(hn, wup_ref[:, lo:lo + cols])
        buf[SUBLANES:SUBLANES + tm, cols:2 * cols] = _dot(hn, wup_ref[:, ffn + lo:ffn + lo + cols])
        halo_sc[c] = buf[tm:tm + SUBLANES, :]

    def activate(c):
        lo = c * cols
        buf = hbuf_sc.at[c % 2]
        w3 = jnp.concatenate([cw_ref[:, lo:lo + cols], cw_ref[:, ffn + lo:ffn + lo + cols]], axis=1)
        y = jnp.concatenate([cb_ref[:, lo:lo + cols], cb_ref[:, ffn + lo:ffn + lo + cols]], axis=1)
        for k in range(FFN_CONV):
            off = SUBLANES - (FFN_CONV - 1) + k
            y = y + w3[k:k + 1, :] * buf[off:off + tm, :]
        act_sc[:, lo:lo + cols] = (_silu(y[:, :cols]) * y[:, cols:]).astype(BF16)

    up(0)
    for c in range(nchunk):
        if c + 1 < nchunk:
            up(c + 1)
        activate(c)
    y = x_sc[...] + _dot(act_sc[...], wdn_ref[...])
    o_ref[...] = _rms_rows(y, gf_ref[...]) if final_norm else y


def _mixer_out_ffn(res, a, wm, glu, g, w_up, conv_w, conv_b, w_down, gf, final_norm, seq):
    t, d = res.shape
    ka = a.shape[1]
    ffn = w_down.shape[0]
    tm = min(FFN_ROWS, seq)
    kern = functools.partial(_ffn_kernel, tm=tm, ffn=ffn, steps_per_batch=seq // tm,
                             glu=glu, final_norm=final_norm)
    return pl.pallas_call(
        kern,
        out_shape=jax.ShapeDtypeStruct((t, d), F32),
        grid=(t // tm,),
        in_specs=[pl.BlockSpec((tm, d), lambda i: (i, 0)),
                  pl.BlockSpec((tm, ka), lambda i: (i, 0)),
                  _resident(wm.shape, lambda i: (0, 0)),
                  _resident((1, d), lambda i: (0, 0)),
                  _resident((d, 2 * ffn), lambda i: (0, 0)),
                  _resident((FFN_CONV, 2 * ffn), lambda i: (0, 0)),
                  _resident((1, 2 * ffn), lambda i: (0, 0)),
                  _resident((ffn, d), lambda i: (0, 0)),
                  _resident((1, d), lambda i: (0, 0))],
        out_specs=pl.BlockSpec((tm, d), lambda i: (i, 0)),
        scratch_shapes=[pltpu.VMEM((tm, d), F32),
                        pltpu.VMEM((tm, d), BF16),
                        pltpu.VMEM((2, tm + SUBLANES, 2 * FFN_COLS), F32),
                        pltpu.VMEM((ffn // FFN_COLS, SUBLANES, 2 * FFN_COLS), F32),
                        pltpu.VMEM((tm, ffn), BF16)],
        compiler_params=_cparams(("arbitrary",), 52),
        name="mixer_out_ffn",
    )(res, a, wm.astype(BF16), g.reshape(1, d), w_up.astype(BF16), conv_w, conv_b.reshape(1, -1),
      w_down.astype(BF16), gf.reshape(1, d))


def _qkv_kernel(x_ref, g_ref, w_ref, wr_ref, cos_ref, sin_ref, o_ref, hn_sc, *, tn):
    hn_sc[...] = _rms_rows(x_ref[...], g_ref[...]).astype(BF16)
    reps = tn // LANES
    n_rope = wr_ref.shape[1]
    for j in range(w_ref.shape[1] // tn):
        cols = slice(j * tn, (j + 1) * tn)
        y = _dot(hn_sc[...], w_ref[:, cols])
        if j * tn < n_rope:
            cos = jnp.concatenate([cos_ref[...]] * reps, axis=1)
            sin = jnp.concatenate([sin_ref[...]] * reps, axis=1)
            y = y * cos + _dot(hn_sc[...], wr_ref[:, cols]) * sin
        o_ref[:, cols] = y.astype(o_ref.dtype)


def _rope_tables(seq):
    half = ROPE_DIM // 2
    pos = jnp.arange(seq, dtype=F32)
    inv_freq = ROPE_THETA ** (-jnp.arange(0, ROPE_DIM, 2, dtype=F32) / ROPE_DIM)
    ang = pos[:, None] * inv_freq[None, :]
    cos, sin = jnp.cos(ang), jnp.sin(ang)
    lane = jnp.arange(LANES) % DA_HEAD_DIM
    idx = lane % half
    cos_l = jnp.where(lane[None, :] < ROPE_DIM, cos[:, idx], 1.0)
    sin_l = jnp.where(lane[None, :] < ROPE_DIM, sin[:, idx], 0.0)
    return cos_l, sin_l


def _rope_partner_weight(w):
    half = ROPE_DIM // 2
    col = jnp.arange(w.shape[1])
    lane = col % DA_HEAD_DIM
    partner = jnp.where(lane < half, col + half, col - half)
    sign = jnp.where(lane < half, -1.0, jnp.where(lane < ROPE_DIM, 1.0, 0.0))
    return w[:, jnp.clip(partner, 0, w.shape[1] - 1)] * sign[None, :]


def _qkv_proj(x2, g, w_qkv, seq):
    t, d = x2.shape
    width = w_qkv.shape[1] // 3
    scale = DA_HEAD_DIM ** -0.5 * math.log2(math.e)
    w = jnp.concatenate([w_qkv[:, :width] * scale, w_qkv[:, width:]], axis=1).astype(BF16)
    wr = _rope_partner_weight(w[:, :2 * width])
    n = w.shape[1]
    tm = min(512, seq)
    tn = 512
    cos_l, sin_l = _rope_tables(seq)
    spb = seq // tm
    tab = pl.BlockSpec((tm, LANES), lambda i: (i % spb, 0))
    return pl.pallas_call(
        functools.partial(_qkv_kernel, tn=tn),
        out_shape=jax.ShapeDtypeStruct((t, n), BF16),
        grid=(t // tm,),
        in_specs=[pl.BlockSpec((tm, d), lambda i: (i, 0)),
                  _resident((1, d), lambda i: (0, 0)),
                  _resident((d, n), lambda i: (0, 0)),
                  _resident((d, 2 * width), lambda i: (0, 0)),
                  tab, tab],
        out_specs=pl.BlockSpec((tm, n), lambda i: (i, 0)),
        scratch_shapes=[pltpu.VMEM((tm, d), BF16)],
        compiler_params=_cparams(("parallel",), 40),
        name="qkv_rope",
    )(x2, g.reshape(1, d), w, wr, cos_l, sin_l)


def _attn_kernel(q_ref, k_ref, v_ref, lq1_ref, lk1_ref, lq2_ref, lk2_ref, gain_ref, o_ref,
                 vt_sc, qt_sc, s_sc, mx_sc, acc_sc, m_sc, l_sc, *, tq, seq, lambda_init, unroll):
    qi = pl.program_id(2)
    nkv = seq // tq

    @pl.when(qi == 0)
    def _():
        def tr(t, carry):
            st = pl.multiple_of(t * tq, tq)
            vt_sc[t] = v_ref[pl.ds(st, tq), :].astype(F32).T.astype(BF16)
            return carry
        lax.fori_loop(0, nkv, tr, 0)

    qt = q_ref[...].astype(F32).T
    row = lax.broadcasted_iota(jnp.int32, qt.shape, 0)
    qt_sc[0] = jnp.where(row < DA_HEAD_DIM, qt, 0.0).astype(BF16)
    qt_sc[1] = jnp.where(row >= DA_HEAD_DIM, qt, 0.0).astype(BF16)

    m_sc[...] = jnp.full_like(m_sc, -jnp.inf)
    l_sc[...] = jnp.zeros_like(l_sc)
    acc_sc[...] = jnp.zeros_like(acc_sc)

    def scores(t, slot, masked):
        st = pl.multiple_of(t * tq, tq)
        kt = k_ref[pl.ds(st, tq), :]
        if masked:
            kc = _idiv(lax.broadcasted_iota(jnp.int32, (tq, tq), 0), CHUNK)
            qc = _idiv(lax.broadcasted_iota(jnp.int32, (tq, tq), 1), CHUNK)
            keep = kc <= qc
        for c in range(2):
            s = _dot(kt, qt_sc[c])
            if masked:
                s = jnp.where(keep, s, -jnp.inf)
            s_sc[slot, c] = s
            mx_sc[slot, c] = jnp.max(s, axis=0, keepdims=True)

    def accumulate(t, slot):
        vt = vt_sc[t]
        for c in range(2):
            m_prev = m_sc[c]
            m_new = jnp.maximum(m_prev, mx_sc[slot, c])
            alpha = jnp.exp2(m_prev - m_new)
            p = jnp.exp2(s_sc[slot, c] - m_new)
            l_sc[c] = alpha * l_sc[c] + jnp.sum(p, axis=0, keepdims=True)
            acc_sc[c] = alpha * acc_sc[c] + _dot(vt, p.astype(BF16))
            m_sc[c] = m_new

    scores(qi, 0, True)

    def run(width):
        def body(u, carry):
            for j in range(width):
                t = width * u + j
                scores(t, (j + 1) % 2, False)
                accumulate(jnp.where(t == 0, qi, t - 1), j % 2)
            return carry
        return body

    nmain = qi // unroll
    npair = qi // 2
    lax.fori_loop(0, nmain, run(unroll), 0)
    lax.fori_loop(nmain * (unroll // 2), npair, run(2), 0)
    odd = qi % 2 == 1

    @pl.when(odd)
    def _():
        scores(qi - 1, 1, False)
        accumulate(jnp.where(qi == 1, qi, qi - 2), 0)
        accumulate(qi - 1, 1)

    @pl.when(jnp.logical_not(odd))
    def _():
        accumulate(jnp.where(qi == 0, qi, qi - 1), 0)

    lam =(jnp.exp(jnp.sum(lq1_ref[...] * lk1_ref[...], axis=-1, keepdims=True))
           - jnp.exp(jnp.sum(lq2_ref[...] * lk2_ref[...], axis=-1, keepdims=True)) + lambda_init)
    o = acc_sc[0] / l_sc[0] - lam * (acc_sc[1] / l_sc[1])
    ms = jnp.mean(o * o, axis=0, keepdims=True)
    on = (o * lax.rsqrt(ms + EPS)).T
    o_ref[...] = (on * gain_ref[...] * (1.0 - lambda_init)).astype(o_ref.dtype)


def _attn_pair_kernel(q_ref, k_ref, v_ref, lq1_ref, lk1_ref, lq2_ref, lk2_ref, gain_ref, o_ref,
                      vt_sc, qt_sc, s_sc, mx_sc, acc_sc, m_sc, l_sc, *, tq, seq, lambda_init, unroll):
    i = pl.program_id(2)
    nkv = seq // tq

    @pl.when(i == 0)
    def _():
        def tr(t, carry):
            st = pl.multiple_of(t * tq, tq)
            vt_sc[t] = v_ref[pl.ds(st, tq), :].astype(F32).T.astype(BF16)
            return carry
        lax.fori_loop(0, nkv, tr, 0)

    for w in range(2):
        qt = q_ref[w * tq:(w + 1) * tq, :].astype(F32).T
        row = lax.broadcasted_iota(jnp.int32, qt.shape, 0)
        qt_sc[w, 0] = jnp.where(row < DA_HEAD_DIM, qt, 0.0).astype(BF16)
        qt_sc[w, 1] = jnp.where(row >= DA_HEAD_DIM, qt, 0.0).astype(BF16)

    def reset():
        m_sc[...] = jnp.full_like(m_sc, -jnp.inf)
        l_sc[...] = jnp.zeros_like(l_sc)
        acc_sc[...] = jnp.zeros_like(acc_sc)

    def scores(w, t, slot, masked):
        st = pl.multiple_of(t * tq, tq)
        kt = k_ref[pl.ds(st, tq), :]
        if masked:
            kc = _idiv(lax.broadcasted_iota(jnp.int32, (tq, tq), 0), CHUNK)
            qc = _idiv(lax.broadcasted_iota(jnp.int32, (tq, tq), 1), CHUNK)
            keep = kc <= qc
        for c in range(2):
            s = _dot(kt, qt_sc[w, c])
            if masked:
                s = jnp.where(keep, s, -jnp.inf)
            s_sc[slot, c] = s
            mx_sc[slot, c] = jnp.max(s, axis=0, keepdims=True)

    def accumulate(t, slot):
        vt = vt_sc[t]
        for c in range(2):
            m_prev = m_sc[c]
            m_new = jnp.maximum(m_prev, mx_sc[slot, c])
            alpha = jnp.exp2(m_prev - m_new)
            p = jnp.exp2(s_sc[slot, c] - m_new)
            l_sc[c] = alpha * l_sc[c] + jnp.sum(p, axis=0, keepdims=True)
            acc_sc[c] = alpha * acc_sc[c] + _dot(vt, p.astype(BF16))
            m_sc[c] = m_new

    lam = (jnp.exp(jnp.sum(lq1_ref[...] * lk1_ref[...], axis=-1, keepdims=True))
           - jnp.exp(jnp.sum(lq2_ref[...] * lk2_ref[...], axis=-1, keepdims=True)) + lambda_init)

    def finalize(w):
        o = acc_sc[0] / l_sc[0] - lam * (acc_sc[1] / l_sc[1])
        ms = jnp.mean(o * o, axis=0, keepdims=True)
        on = (o * lax.rsqrt(ms + EPS)).T
        o_ref[w * tq:(w + 1) * tq, :] = (on * gain_ref[...] * (1.0 - lambda_init)).astype(o_ref.dtype)

    def run(w, qd, width, swap):
        def body(u, carry):
            for j in range(width):
                t = width * u + j
                scores(w, t, (j + 1 + swap) % 2, False)
                accumulate(jnp.where(t == 0, qd, t - 1), (j + swap) % 2)
            return carry
        return body

    qa = 2 * i
    qb = qa + 1
    reset()
    scores(0, qa, 0, True)
    na = qa // unroll
    lax.fori_loop(0, na, run(0, qa, unroll, 0), 0)
    lax.fori_loop(na * (unroll // 2), qa // 2, run(0, qa, 2, 0), 0)
    scores(1, qb, 1, True)
    accumulate(jnp.where(qa == 0, qa, qa - 1), 0)
    finalize(0)

    reset()
    nb = qb // unroll
    lax.fori_loop(0, nb, run(1, qb, unroll, 1), 0)
    lax.fori_loop(nb * (unroll // 2), qb // 2, run(1, qb, 2, 1), 0)
    scores(1, qb - 1, 0, False)
    accumulate(jnp.where(qb == 1, qb, qb - 2), 1)
    accumulate(qb - 1, 0)
    finalize(1)


def _diff_attention_core(qkv, lq1, lk1, lq2, lk2, sub_gain, lambda_init, batch, seq, pair):
    width = DA_HEADS * DA_VALUE_DIM
    qkv3 = qkv.reshape(batch, seq, 3 * width)
    tq = min(512, seq)
    nq = 2 if pair else 1
    kern = functools.partial(_attn_pair_kernel if pair else _attn_kernel,
                             tq=tq, seq=seq, lambda_init=lambda_init, unroll=ATTN_UNROLL)
    vec = pl.BlockSpec((1, DA_HEAD_DIM), lambda b, h, i: (0, 0))
    qt_shape = (2, 2, DA_VALUE_DIM, tq) if pair else (2, DA_VALUE_DIM, tq)
    return pl.pallas_call(
        kern,
        out_shape=jax.ShapeDtypeStruct((batch, seq, width), BF16),
        grid=(batch, DA_HEADS, seq // (nq * tq)),
        in_specs=[pl.BlockSpec((None, nq * tq, DA_VALUE_DIM), lambda b, h, i: (b, i, h)),
                  pl.BlockSpec((None, seq, DA_VALUE_DIM), lambda b, h, i: (b, 0, DA_HEADS + h)),
                  pl.BlockSpec((None, seq, DA_VALUE_DIM), lambda b, h, i: (b, 0, 2 * DA_HEADS + h)),
                  vec, vec, vec, vec,
                  pl.BlockSpec((1, DA_VALUE_DIM), lambda b, h, i: (0, 0))],
        out_specs=pl.BlockSpec((None, nq * tq, DA_VALUE_DIM), lambda b, h, i: (b, i, h)),
        scratch_shapes=[pltpu.VMEM((seq // tq, DA_VALUE_DIM, tq), BF16),
                        pltpu.VMEM(qt_shape, BF16),
                        pltpu.VMEM((2, 2, tq, tq), F32),
                        pltpu.VMEM((2, 2, 1, tq), F32),
                        pltpu.VMEM((2, DA_VALUE_DIM, tq), F32),
                        pltpu.VMEM((2, 1, tq), F32),
                        pltpu.VMEM((2, 1, tq), F32)],
        compiler_params=_cparams(("parallel", "parallel", "arbitrary"), 48),
        name="diff_attention",
    )(qkv3, qkv3, qkv3, lq1.reshape(1, -1), lk1.reshape(1, -1), lq2.reshape(1, -1),
      lk2.reshape(1, -1), sub_gain.reshape(1, -1)).reshape(batch * seq, width)


def _ssd_kernel(xbc_ref, z_ref, dt_ref, dtb_ref, alog_ref, dsk_ref, ng_ref,
                tril_ref, e128_ref, e64_ref, y_ref, state_sc, *, steps_per_batch, inner):
    L = SSD_CHUNK
    G = SSM_GROUPS
    N = SSM_STATE
    gw = inner // G
    R = gw // SSM_HEAD_DIM

    @pl.when(pl.program_id(0) % steps_per_batch == 0)
    def _():
        state_sc[...] = jnp.zeros_like(state_sc)

    xbc = xbc_ref[...].astype(F32)

    x = dt_ref[...] + dtb_ref[...]
    dt = jnp.maximum(x, 0.0) + jnp.log(1.0 + jnp.exp(-jnp.abs(x)))
    adt = dt * (-jnp.exp(alog_ref[...]))

    acum = _dot3_lhs01(tril_ref[...], adt)
    acol128 = _dot2_rhs01(acum, e128_ref[...])
    acol64 = _dot2_rhs01(acum, e64_ref[...])
    dt64 = _dot(dt.astype(BF16), e64_ref[...])

    ri = lax.broadcasted_iota(jnp.int32, acol128.shape, 0)
    ci = lax.broadcasted_iota(jnp.int32, acol128.shape, 1) & (L - 1)
    arow = jnp.sum(jnp.where(ri == ci, acol128, 0.0), axis=0, keepdims=True)
    decay_in = jnp.exp(jnp.where(ci <= ri, acol128 - arow, -jnp.inf))

    xs = xbc[:, :inner]
    xdt = xs * dt64
    last = acol64[L - 1:L, :]
    xdo = (xdt * jnp.exp(last - acol64)).astype(BF16)
    xdt = xdt.astype(BF16)
    e_a = jnp.exp(acol64)
    e_last = jnp.exp(last)

    rr = _idiv(lax.broadcasted_iota(jnp.int32, (R * L, gw), 0), L)
    cc = _idiv(lax.broadcasted_iota(jnp.int32, (R * L, gw), 1), SSM_HEAD_DIM)
    blockdiag = rr == cc

    for g in range(G):
        bg = xbc[:, inner + g * N: inner + (g + 1) * N]
        cg = xbc[:, inner + G * N + g * N: inner + G * N + (g + 1) * N].astype(BF16)
        bgt = bg.T.astype(BF16)
        cb = _dot(cg, bgt)
        cbt = jnp.concatenate([cb] * R, axis=1)
        mg = (cbt * decay_in[:, g * R * L:(g + 1) * R * L]).astype(BF16)
        xg = xdt[:, g * gw:(g + 1) * gw]
        xbd = jnp.where(blockdiag, jnp.concatenate([xg] * R, axis=0), jnp.zeros((), BF16))
        y = _dot(mg, xbd)
        st = state_sc[g]
        y = y + _dot(cg, st.astype(BF16)) * e_a[:, g * gw:(g + 1) * gw]
        state_sc[g] = st * e_last[:, g * gw:(g + 1) * gw] + _dot(bgt, xdo[:, g * gw:(g + 1) * gw])
        y = y + dsk_ref[:, g * gw:(g + 1) * gw] * xs[:, g * gw:(g + 1) * gw]
        y = y * _silu(z_ref[:, g * gw:(g + 1) * gw].astype(F32))
        ms = jnp.mean(y * y, axis=-1, keepdims=True)
        y_ref[:, g * gw:(g + 1) * gw] = (y * lax.rsqrt(ms + EPS) * ng_ref[:, g * gw:(g + 1) * gw]).astype(y_ref.dtype)


def _mamba2_mixer(x2, g, w_in, conv_w, conv_b, dt_bias, a_log, d_skip, norm_gain, seq):
    t, d = x2.shape
    nh = dt_bias.shape[0]
    inner = nh * SSM_HEAD_DIM
    conv_dim = conv_w.shape[1]
    L = SSD_CHUNK
    w_xz = jnp.concatenate([w_in[:, inner:inner + conv_dim], w_in[:, :inner]], axis=1).astype(BF16)
    w_dt = jnp.pad(w_in[:, inner + conv_dim:], ((0, 0), (0, LANES - nh))).astype(BF16)
    xz, dtr = _inproj_conv(x2, g, w_xz, w_dt, conv_w, conv_b, seq, 1024, 512)

    pad1 = lambda v: jnp.pad(v, (0, LANES - nh)).reshape(1, LANES)
    rows = jnp.arange(L)
    tril = (rows[None, :] <= rows[:, None]).astype(BF16)
    head = jnp.arange(LANES)
    e128 = (head[:, None] == (jnp.arange(nh * L) // L)[None, :]).astype(BF16)
    e64 = (head[:, None] == (jnp.arange(inner) // SSM_HEAD_DIM)[None, :]).astype(BF16)
    dsk = jnp.repeat(d_skip, SSM_HEAD_DIM).reshape(1, inner)

    nblk_x = conv_dim // inner
    kern = functools.partial(_ssd_kernel, steps_per_batch=seq // L, inner=inner)
    const = lambda shape: _resident(shape, lambda i: (0, 0))
    y = pl.pallas_call(
        kern,
        out_shape=jax.ShapeDtypeStruct((t, inner), BF16),
        grid=(t // L,),
        in_specs=[pl.BlockSpec((L, conv_dim), lambda i: (i, 0)),
                  pl.BlockSpec((L, inner), lambda i: (i, nblk_x)),
                  pl.BlockSpec((L, LANES), lambda i: (i, 0)),
                  const((1, LANES)), const((1, LANES)),
                  const((1, inner)), const((1, inner)),
                  const((L, L)), const((LANES, nh * L)), const((LANES, inner))],
        out_specs=pl.BlockSpec((L, inner), lambda i: (i, 0)),
        scratch_shapes=[pltpu.VMEM((SSM_GROUPS, SSM_STATE, inner // SSM_GROUPS), F32)],
        compiler_params=_cparams(("arbitrary",), 48),
        name="ssd_scan",
    )(xz, xz, dtr, pad1(dt_bias), pad1(a_log), dsk,
      norm_gain.reshape(1, -1), tril, e128, e64)
    return y


def _s5_tables(a_re, a_im, log_dt, b_re, b_im, c_re, c_im, d_skip):
    L = S5_CHUNK
    G, P = a_re.shape
    K = b_re.shape[-1]
    lam = lax.complex(jnp.minimum(a_re, -1e-4), a_im)
    delta = jnp.exp(log_dt)[:, None]
    lam_bar = jnp.exp(lam * delta)
    b_bar = ((lam_bar - 1.0) / lam)[..., None] * lax.complex(b_re, b_im)
    c = lax.complex(c_re, c_im)
    tau = jnp.arange(L + 1, dtype=F32)
    pw = jnp.exp(tau[:, None, None] * (lam * delta)[None])

    gs = LANES // K
    ns = G // gs
    eye = jnp.eye(gs, dtype=F32)
    kt = jnp.einsum("gkp,tgp,gpi->gtki", c, pw[:L], b_bar).real
    lag = jnp.arange(L)[None, :] - jnp.arange(L)[:, None]
    blk = jnp.einsum("sgtoi,gh->stgiho", kt.reshape(ns, gs, L, K, K), eye).reshape(ns, L, LANES, LANES)
    m = jnp.where((lag >= 0)[None, :, :, None, None], blk[:, jnp.maximum(lag, 0)], 0.0)
    m = m.transpose(0, 1, 3, 2, 4).reshape(ns, L * LANES, L * LANES)

    bst = pw[:L][::-1].transpose(1, 0, 2)[:, :, None, :] * b_bar.transpose(0, 2, 1)[:, None, :, :]
    bst = bst.reshape(ns, gs, L, K, P)
    put_b = lambda v: jnp.einsum("sgjip,gh->sjgihp", v, eye).reshape(ns, L * LANES, gs * P)
    w = c.transpose(0, 2, 1)[:, :, None, :] * pw[1:].transpose(1, 2, 0)[:, :, :, None]
    w = w.reshape(ns, gs, P, L, K)
    put_c = lambda v: jnp.einsum("sgpto,gh->sgptho", v, eye).reshape(ns, gs * P, L * LANES)
    lam_l = pw[L].reshape(ns, 1, gs * P)
    return dict(
        m=m.astype(BF16),
        bst_re=put_b(bst.real).astype(BF16), bst_im=put_b(bst.imag).astype(BF16),
        cst_re=put_c(w.real).astype(BF16), cst_im=put_c(-w.imag).astype(BF16),
        lam_re=lam_l.real, lam_im=lam_l.imag, dsk=d_skip.reshape(ns, 1, LANES))


def _s5_kernel(u_ref, m_ref, bre_ref, bim_ref, cre_ref, cim_ref, lre_ref, lim_ref, dsk_ref, o_ref,
               a_sc, s_re_sc, s_im_sc, in_re_sc, in_im_sc, carry_sc, *, rows, batch):
    L = S5_CHUNK

    @pl.when(pl.program_id(1) == 0)
    def _():
        carry_sc[...] = jnp.zeros_like(carry_sc)

    for b in range(batch):
        for j in range(L):
            a_sc[b * rows:(b + 1) * rows, j * LANES:(j + 1) * LANES] = (
                u_ref[b, pl.ds(j, rows, stride=L), :].astype(BF16))

    s_re_sc[...] = _dot(a_sc[...], bre_ref[0])
    s_im_sc[...] = _dot(a_sc[...], bim_ref[0])
    lre = lre_ref[0]
    lim = lim_ref[0]

    def body(c, carry):
        out = []
        for b in range(batch):
            sre, sim = carry[b]
            r = b * rows + c
            in_re_sc[pl.ds(r, 1), :] = sre
            in_im_sc[pl.ds(r, 1), :] = sim
            out.append((lre * sre - lim * sim + s_re_sc[pl.ds(r, 1), :],
                        lre * sim + lim * sre + s_im_sc[pl.ds(r, 1), :]))
        return tuple(out)

    init = tuple((carry_sc[2 * b:2 * b + 1, :], carry_sc[2 * b + 1:2 * b + 2, :]) for b in range(batch))
    last = lax.fori_loop(0, rows, body, init)
    for b in range(batch):
        carry_sc[2 * b:2 * b + 1, :] = last[b][0]
        carry_sc[2 * b + 1:2 * b + 2, :] = last[b][1]

    sre = in_re_sc[...].astype(BF16)
    sim = in_im_sc[...].astype(BF16)
    pairw = 2 * LANES
    for p in range(L // 2):
        cols = slice(p * pairw, (p + 1) * pairw)
        kk = (p + 1) * pairw
        y2 = (_dot(a_sc[:, :kk], m_ref[0, :kk, cols])
              + _dot(sre, cre_ref[0, :, cols]) + _dot(sim, cim_ref[0, :, cols]))
        for h in range(2):
            t = 2 * p + h
            for b in range(batch):
                y = (y2[b * rows:(b + 1) * rows, h * LANES:(h + 1) * LANES]
                     + dsk_ref[0] * u_ref[b, pl.ds(t, rows, stride=L), :])
                inner = math.sqrt(2.0 / math.pi) * (y + 0.044715 * (y * y * y))
                o_ref[b, pl.ds(t, rows, stride=L), :] = 0.5 * y * (1.0 + jnp.tanh(inner))


def _s5_mixer(x2, g, a_re, a_im, log_dt, b_re, b_im, c_re, c_im, d_skip, seq):
    t, d = x2.shape
    L = S5_CHUNK
    G, K = d_skip.shape
    tab = _s5_tables(a_re, a_im, log_dt, b_re, b_im, c_re, c_im, d_skip)
    u = _rmsnorm(x2, g, F32)
    ns = d // LANES
    sw = (LANES // K) * S5_STATE
    batch = t // seq
    assert 2 * batch <= SUBLANES
    steps = min(S5_STEPS, seq)
    rows = steps // L
    lw = L * LANES
    kern = functools.partial(_s5_kernel, rows=rows, batch=batch)
    slab = lambda shape: pl.BlockSpec(shape, lambda s, i: (s, 0, 0))
    ya = pl.pallas_call(
        kern,
        out_shape=jax.ShapeDtypeStruct((batch, seq, d), F32),
        grid=(ns, seq // steps),
        in_specs=[pl.BlockSpec((batch, steps, LANES), lambda s, i: (0, i, s)),
                  slab((1, lw, lw)), slab((1, lw, sw)), slab((1, lw, sw)),
                  slab((1, sw, lw)), slab((1, sw, lw)),
                  slab((1, 1, sw)), slab((1, 1, sw)), slab((1, 1, LANES))],
        out_specs=pl.BlockSpec((batch, steps, LANES), lambda s, i: (0, i, s)),
        scratch_shapes=[pltpu.VMEM((batch * rows, lw), BF16)] + [pltpu.VMEM((batch * rows, sw), F32)] * 4
                       + [pltpu.VMEM((SUBLANES, sw), F32)],
        compiler_params=_cparams(("parallel", "arbitrary"), 48),
        name="s5_scan",
    )(u.reshape(batch, seq, d), tab["m"], tab["bst_re"], tab["bst_im"], tab["cst_re"], tab["cst_im"],
      tab["lam_re"], tab["lam_im"], tab["dsk"]).reshape(t, d)
    return ya


def kernel(x, norm_mix, norm_ffn, norm_final, attn_w_qkv, attn_w_o, attn_lambda_q1, attn_lambda_k1, attn_lambda_q2, attn_lambda_k2, attn_sub_gain, ssm_w_in, ssm_conv_w, ssm_conv_b, ssm_dt_bias, ssm_a_log, ssm_d, ssm_norm_gain, ssm_w_out, s5_a_re, s5_a_im, s5_log_dt, s5_b_re, s5_b_im, s5_c_re, s5_c_im, s5_d, s5_w_glu, ffn_w_up, ffn_conv_w, ffn_conv_b, ffn_w_down):
    batch, seq, d = x.shape
    depth = norm_mix.shape[0]
    x2 = x.reshape(batch * seq, d)
    for i in range(depth):
        kind = i % N_MIXERS
        j = i // N_MIXERS
        if kind == 0:
            lambda_init = 0.8 - 0.6 * math.exp(-0.3 * i)
            qkv = _qkv_proj(x2, norm_mix[i], attn_w_qkv[j], seq)
            a = _diff_attention_core(qkv, attn_lambda_q1[j], attn_lambda_k1[j], attn_lambda_q2[j],
                                     attn_lambda_k2[j], attn_sub_gain[j], lambda_init, batch, seq,
                                     ATTN_PAIR[j])
            wm = attn_w_o[j]
        elif kind == 1:
            a = _mamba2_mixer(x2, norm_mix[i], ssm_w_in[j], ssm_conv_w[j], ssm_conv_b[j], ssm_dt_bias[j],
                              ssm_a_log[j], ssm_d[j], ssm_norm_gain[j], seq)
            wm = ssm_w_out[j]
        else:
            a = _s5_mixer(x2, norm_mix[i], s5_a_re[j], s5_a_im[j], s5_log_dt[j], s5_b_re[j], s5_b_im[j],
                          s5_c_re[j], s5_c_im[j], s5_d[j], seq)
            wm = s5_w_glu[j]
        x2 = _mixer_out_ffn(x2, a, wm, kind == 2, norm_ffn[i], ffn_w_up[i], ffn_conv_w[i], ffn_conv_b[i],
                            ffn_w_down[i], norm_final, i == depth - 1, seq)
    return x2.reshape(batch, seq, d)
```

```python
import functools
import math

import jax
import jax.numpy as jnp
from jax import lax
from jax.experimental import pallas as pl
from jax.experimental.pallas import tpu as pltpu

F32 = jnp.float32
BF16 = jnp.bfloat16

EPS = 1e-6
N_MIXERS = 3
CHUNK = 64
DA_HEADS = 8
DA_HEAD_DIM = 64
DA_VALUE_DIM = 2 * DA_HEAD_DIM
ROPE_THETA = 500000.0
ROPE_DIM = DA_HEAD_DIM // 4
SSM_HEAD_DIM = 64
SSM_GROUPS = 8
SSM_STATE = 128
SSM_CONV = 4
SSD_CHUNK = 128
S5_GROUP = 16
S5_STATE = 64
S5_CHUNK = 8
S5_STEPS = 2048
ATTN_UNROLL = 4
ATTN_TILE = 512
FFN_CONV = 3
FFN_COLS = 256
FFN_ROWS = 512
LANES = 128
SUBLANES = 8
MIB = 1024 * 1024


def _cparams(sem, vmem_mib):
    return pltpu.CompilerParams(dimension_semantics=sem, vmem_limit_bytes=vmem_mib * MIB)


def _resident(shape, index_map):
    return pl.BlockSpec(shape, index_map, pipeline_mode=pl.Buffered(1))


def _rms_rows(x, g):
    ms = jnp.mean(x * x, axis=-1, keepdims=True)
    return x * lax.rsqrt(ms + EPS) * g


def _sigmoid(x):
    return 0.5 + 0.5 * jnp.tanh(0.5 * x)


def _silu(x):
    h = 0.5 * x
    return h + h * jnp.tanh(h)


def _idiv(v, n):
    return lax.shift_right_logical(v, int(math.log2(n)))


def _split3(x):
    hi = x.astype(BF16)
    r = x - hi.astype(F32)
    mid = r.astype(BF16)
    lo = (r - mid.astype(F32)).astype(BF16)
    return hi, mid, lo


def _dot(a, b):
    return jnp.dot(a, b, preferred_element_type=F32)


def _dot2_rhs01(x, e):
    hi = x.astype(BF16)
    mid = (x - hi.astype(F32)).astype(BF16)
    return _dot(hi, e) + _dot(mid, e)


def _dot3_lhs01(e, x):
    hi, mid, lo = _split3(x)
    return _dot(e, hi) + _dot(e, mid) + _dot(e, lo)


def _norm_kernel(x_ref, g_ref, o_ref):
    o_ref[...] = _rms_rows(x_ref[...], g_ref[...]).astype(o_ref.dtype)


def _rmsnorm(x2, g, out_dtype):
    t, d = x2.shape
    tm = min(512, t)
    return pl.pallas_call(
        _norm_kernel,
        out_shape=jax.ShapeDtypeStruct((t, d), out_dtype),
        grid=(t // tm,),
        in_specs=[pl.BlockSpec((tm, d), lambda i: (i, 0)),
                  pl.BlockSpec((1, d), lambda i: (0, 0))],
        out_specs=pl.BlockSpec((tm, d), lambda i: (i, 0)),
        compiler_params=_cparams(("parallel",), 32),
        name="rmsnorm",
    )(x2, g.reshape(1, d))


def _inproj_conv_kernel(x_ref, g_ref, w_ref, wdt_ref, cw_ref, cb_ref, o_ref, dt_ref, hn_sc, hbuf_sc, halo_sc,
                        *, tm, tn, n_conv, steps_per_batch):
    taps = cw_ref.shape[0]

    @pl.when(pl.program_id(0) % steps_per_batch == 0)
    def _():
        halo_sc[...] = jnp.zeros_like(halo_sc)

    hn_sc[...] = _rms_rows(x_ref[...], g_ref[...]).astype(BF16)
    dt_ref[...] = _dot(hn_sc[...], wdt_ref[...])
    for j in range(w_ref.shape[1] // tn):
        cols = slice(j * tn, (j + 1) * tn)
        y = _dot(hn_sc[...], w_ref[:, cols])
        if j * tn < n_conv:
            buf = hbuf_sc.at[j % 2]
            buf[0:SUBLANES, :] = halo_sc[j]
            buf[SUBLANES:SUBLANES + tm, :] = y
            halo_sc[j] = buf[tm:tm + SUBLANES, :]
            y = cb_ref[:, cols]
            for k in range(taps):
                off = SUBLANES - (taps - 1) + k
                y = y + cw_ref[k:k + 1, cols] * buf[off:off + tm, :]
            y = _silu(y)
        o_ref[:, cols] = y.astype(o_ref.dtype)


def _inproj_conv(x2, g, w, w_dt, conv_w, conv_b, seq, tn, tm):
    t, d = x2.shape
    n = w.shape[1]
    ndt = w_dt.shape[1]
    n_conv = conv_w.shape[1]
    tm = min(tm, seq)
    kern = functools.partial(_inproj_conv_kernel, tm=tm, tn=tn, n_conv=n_conv, steps_per_batch=seq // tm)
    return pl.pallas_call(
        kern,
        out_shape=(jax.ShapeDtypeStruct((t, n), BF16), jax.ShapeDtypeStruct((t, ndt), F32)),
        grid=(t // tm,),
        in_specs=[pl.BlockSpec((tm, d), lambda i: (i, 0)),
                  _resident((1, d), lambda i: (0, 0)),
                  _resident((d, n), lambda i: (0, 0)),
                  _resident((d, ndt), lambda i: (0, 0)),
                  _resident(conv_w.shape, lambda i: (0, 0)),
                  _resident((1, n_conv), lambda i: (0, 0))],
        out_specs=(pl.BlockSpec((tm, n), lambda i: (i, 0)), pl.BlockSpec((tm, ndt), lambda i: (i, 0))),
        scratch_shapes=[pltpu.VMEM((tm, d), BF16),
                        pltpu.VMEM((2, tm + SUBLANES, tn), F32),
                        pltpu.VMEM((n_conv // tn, SUBLANES, tn), F32)],
        compiler_params=_cparams(("arbitrary",), 48),
        name="inproj_conv",
    )(x2, g.reshape(1, d), w, w_dt, conv_w, conv_b.reshape(1, -1))


def _ffn_kernel(r_ref, a_ref, wm_ref, g_ref, wup_ref, cw_ref, cb_ref, wdn_ref, gf_ref, o_ref,
                x_sc, hn_sc, hbuf_sc, halo_sc, act_sc, *, tm, ffn, steps_per_batch, glu, final_norm):
    cols = FFN_COLS
    nchunk = ffn // cols

    @pl.when(pl.program_id(0) % steps_per_batch == 0)
    def _():
        halo_sc[...] = jnp.zeros_like(halo_sc)

    a = a_ref[...].astype(BF16)
    if glu:
        n = r_ref.shape[1]
        x_sc[...] = r_ref[...] + _dot(a, wm_ref[:, :n]) * _sigmoid(_dot(a, wm_ref[:, n:]))
    else:
        x_sc[...] = r_ref[...] + _dot(a, wm_ref[...])
    hn_sc[...] = _rms_rows(x_sc[...], g_ref[...]).astype(BF16)

    def up(c):
        lo = c * cols
        buf = hbuf_sc.at[c % 2]
        hn = hn_sc[...]
        buf[0:SUBLANES, :] = halo_sc[c]
        buf[SUBLANES:SUBLANES + tm, 0:cols] = _dot(hn, wup_ref[:, lo:lo + cols])
        buf[SUBLANES:SUBLANES + tm, cols:2 * cols] = _dot(hn, wup_ref[:, ffn + lo:ffn + lo + cols])
        halo_sc[c] = buf[tm:tm + SUBLANES, :]

    def activate(c):
        lo = c * cols
        buf = hbuf_sc.at[c % 2]
        w3 = jnp.concatenate([cw_ref[:, lo:lo + cols], cw_ref[:, ffn + lo:ffn + lo + cols]], axis=1)
        y = jnp.concatenate([cb_ref[:, lo:lo + cols], cb_ref[:, ffn + lo:ffn + lo + cols]], axis=1)
        for k in range(FFN_CONV):
            off = SUBLANES - (FFN_CONV - 1) + k
            y = y + w3[k:k + 1, :] * buf[off:off + tm, :]
        act_sc[:, lo:lo + cols] = (_silu(y[:, :cols]) * y[:, cols:]).astype(BF16)

    up(0)
    for c in range(nchunk):
        if c + 1 < nchunk:
            up(c + 1)
        activate(c)
    y = x_sc[...] + _dot(act_sc[...], wdn_ref[...])
    o_ref[...] = _rms_rows(y, gf_ref[...]) if final_norm else y


def _mixer_out_ffn(res, a, wm, glu, g, w_up, conv_w, conv_b, w_down, gf, final_norm, seq):
    t, d = res.shape
    ka = a.shape[1]
    ffn = w_down.shape[0]
    tm = min(FFN_ROWS, seq)
    kern = functools.partial(_ffn_kernel, tm=tm, ffn=ffn, steps_per_batch=seq // tm,
                             glu=glu, final_norm=final_norm)
    return pl.pallas_call(
        kern,
        out_shape=jax.ShapeDtypeStruct((t, d), F32),
        grid=(t // tm,),
        in_specs=[pl.BlockSpec((tm, d), lambda i: (i, 0)),
                  pl.BlockSpec((tm, ka), lambda i: (i, 0)),
                  _resident(wm.shape, lambda i: (0, 0)),
                  _resident((1, d), lambda i: (0, 0)),
                  _resident((d, 2 * ffn), lambda i: (0, 0)),
                  _resident((FFN_CONV, 2 * ffn), lambda i: (0, 0)),
                  _resident((1, 2 * ffn), lambda i: (0, 0)),
                  _resident((ffn, d), lambda i: (0, 0)),
                  _resident((1, d), lambda i: (0, 0))],
        out_specs=pl.BlockSpec((tm, d), lambda i: (i, 0)),
        scratch_shapes=[pltpu.VMEM((tm, d), F32),
                        pltpu.VMEM((tm, d), BF16),
                        pltpu.VMEM((2, tm + SUBLANES, 2 * FFN_COLS), F32),
                        pltpu.VMEM((ffn // FFN_COLS, SUBLANES, 2 * FFN_COLS), F32),
                        pltpu.VMEM((tm, ffn), BF16)],
        compiler_params=_cparams(("arbitrary",), 52),
        name="mixer_out_ffn",
    )(res, a, wm.astype(BF16), g.reshape(1, d), w_up.astype(BF16), conv_w, conv_b.reshape(1, -1),
      w_down.astype(BF16), gf.reshape(1, d))


def _qkv_kernel(x_ref, g_ref, w_ref, wr_ref, cos_ref, sin_ref, o_ref, hn_sc, *, tn):
    hn_sc[...] = _rms_rows(x_ref[...], g_ref[...]).astype(BF16)
    reps = tn // LANES
    n_rope = wr_ref.shape[1]
    for j in range(w_ref.shape[1] // tn):
        cols = slice(j * tn, (j + 1) * tn)
        y = _dot(hn_sc[...], w_ref[:, cols])
        if j * tn < n_rope:
            cos = jnp.concatenate([cos_ref[...]] * reps, axis=1)
            sin = jnp.concatenate([sin_ref[...]] * reps, axis=1)
            y = y * cos + _dot(hn_sc[...], wr_ref[:, cols]) * sin
        o_ref[:, cols] = y.astype(o_ref.dtype)


def _rope_tables(seq):
    half = ROPE_DIM // 2
    pos = jnp.arange(seq, dtype=F32)
    inv_freq = ROPE_THETA ** (-jnp.arange(0, ROPE_DIM, 2, dtype=F32) / ROPE_DIM)
    ang = pos[:, None] * inv_freq[None, :]
    cos, sin = jnp.cos(ang), jnp.sin(ang)
    lane = jnp.arange(LANES) % DA_HEAD_DIM
    idx = lane % half
    cos_l = jnp.where(lane[None, :] < ROPE_DIM, cos[:, idx], 1.0)
    sin_l = jnp.where(lane[None, :] < ROPE_DIM, sin[:, idx], 0.0)
    return cos_l, sin_l


def _rope_partner_weight(w):
    half = ROPE_DIM // 2
    col = jnp.arange(w.shape[1])
    lane = col % DA_HEAD_DIM
    partner = jnp.where(lane < half, col + half, col - half)
    sign = jnp.where(lane < half, -1.0, jnp.where(lane < ROPE_DIM, 1.0, 0.0))
    return w[:, jnp.clip(partner, 0, w.shape[1] - 1)] * sign[None, :]


def _qkv_proj(x2, g, w_qkv, seq):
    t, d = x2.shape
    width = w_qkv.shape[1] // 3
    scale = DA_HEAD_DIM ** -0.5 * math.log2(math.e)
    w = jnp.concatenate([w_qkv[:, :width] * scale, w_qkv[:, width:]], axis=1).astype(BF16)
    wr = _rope_partner_weight(w[:, :2 * width])
    n = w.shape[1]
    tm = min(512, seq)
    tn = 512
    cos_l, sin_l = _rope_tables(seq)
    spb = seq // tm
    tab = pl.BlockSpec((tm, LANES), lambda i: (i % spb, 0))
    return pl.pallas_call(
        functools.partial(_qkv_kernel, tn=tn),
        out_shape=jax.ShapeDtypeStruct((t, n), BF16),
        grid=(t // tm,),
        in_specs=[pl.BlockSpec((tm, d), lambda i: (i, 0)),
                  _resident((1, d), lambda i: (0, 0)),
                  _resident((d, n), lambda i: (0, 0)),
                  _resident((d, 2 * width), lambda i: (0, 0)),
                  tab, tab],
        out_specs=pl.BlockSpec((tm, n), lambda i: (i, 0)),
        scratch_shapes=[pltpu.VMEM((tm, d), BF16)],
        compiler_params=_cparams(("parallel",), 40),
        name="qkv_rope",
    )(x2, g.reshape(1, d), w, wr, cos_l, sin_l)


def _attn_kernel(q_ref, k_ref, v_ref, lq1_ref, lk1_ref, lq2_ref, lk2_ref, gain_ref, o_ref,
                 vt_sc, qt_sc, s_sc, mx_sc, acc_sc, m_sc, l_sc, *, tq, seq, lambda_init, unroll):
    i = pl.program_id(2)
    nkv = seq // tq

    @pl.when(i == 0)
    def _():
        def tr(t, carry):
            st = pl.multiple_of(t * tq, tq)
            vt_sc[t] = v_ref[pl.ds(st, tq), :].astype(F32).T.astype(BF16)
            return carry
        lax.fori_loop(0, nkv, tr, 0)

    for w in range(2):
        qt = q_ref[w * tq:(w + 1) * tq, :].astype(F32).T
        row = lax.broadcasted_iota(jnp.int32, qt.shape, 0)
        qt_sc[w, 0] = jnp.where(row < DA_HEAD_DIM, qt, 0.0).astype(BF16)
        qt_sc[w, 1] = jnp.where(row >= DA_HEAD_DIM, qt, 0.0).astype(BF16)

    def reset():
        m_sc[...] = jnp.full_like(m_sc, -jnp.inf)
        l_sc[...] = jnp.zeros_like(l_sc)
        acc_sc[...] = jnp.zeros_like(acc_sc)

    def scores(w, t, slot, masked):
        st = pl.multiple_of(t * tq, tq)
        kt = k_ref[pl.ds(st, tq), :]
        if masked:
            kc = _idiv(lax.broadcasted_iota(jnp.int32, (tq, tq), 0), CHUNK)
            qc = _idiv(lax.broadcasted_iota(jnp.int32, (tq, tq), 1), CHUNK)
            keep = kc <= qc
        for c in range(2):
            s = _dot(kt, qt_sc[w, c])
            if masked:
                s = jnp.where(keep, s, -jnp.inf)
            s_sc[slot, c] = s
            mx_sc[slot, c] = jnp.max(s, axis=0, keepdims=True)

    def accumulate(t, slot):
        vt = vt_sc[t]
        for c in range(2):
            m_prev = m_sc[c]
            m_new = jnp.maximum(m_prev, mx_sc[slot, c])
            alpha = jnp.exp2(m_prev - m_new)
            p = jnp.exp2(s_sc[slot, c] - m_new)
            l_sc[c] = alpha * l_sc[c] + jnp.sum(p, axis=0, keepdims=True)
            acc_sc[c] = alpha * acc_sc[c] + _dot(vt, p.astype(BF16))
            m_sc[c] = m_new

    lam = (jnp.exp(jnp.sum(lq1_ref[...] * lk1_ref[...], axis=-1, keepdims=True))
           - jnp.exp(jnp.sum(lq2_ref[...] * lk2_ref[...], axis=-1, keepdims=True)) + lambda_init)

    def finalize(w):
        o = acc_sc[0] / l_sc[0] - lam * (acc_sc[1] / l_sc[1])
        ms = jnp.mean(o * o, axis=0, keepdims=True)
        on = (o * lax.rsqrt(ms + EPS)).T
        o_ref[w * tq:(w + 1) * tq, :] = (on * gain_ref[...] * (1.0 - lambda_init)).astype(o_ref.dtype)

    def run(w, qd, width, swap):
        def body(u, carry):
            for j in range(width):
                t = width * u + j
                scores(w, t, (j + 1 + swap) % 2, False)
                accumulate(jnp.where(t == 0, qd, t - 1), (j + swap) % 2)
            return carry
        return body

    qa = 2 * i
    qb = qa + 1
    reset()
    scores(0, qa, 0, True)
    na = qa // unroll
    lax.fori_loop(0, na, run(0, qa, unroll, 0), 0)
    lax.fori_loop(na * (unroll // 2), qa // 2, run(0, qa, 2, 0), 0)
    scores(1, qb, 1, True)
    accumulate(jnp.where(qa == 0, qa, qa - 1), 0)
    finalize(0)

    reset()
    nb = qb // unroll
    lax.fori_loop(0, nb, run(1, qb, unroll, 1), 0)
    lax.fori_loop(nb * (unroll // 2), qb // 2, run(1, qb, 2, 1), 0)
    scores(1, qb - 1, 0, False)
    accumulate(jnp.where(qb == 1, qb, qb - 2), 1)
    accumulate(qb - 1, 0)
    finalize(1)


def _diff_attention_core(qkv, lq1, lk1, lq2, lk2, sub_gain, lambda_init, batch, seq):
    width = DA_HEADS * DA_VALUE_DIM
    qkv3 = qkv.reshape(batch, seq, 3 * width)
    tq = min(ATTN_TILE, seq // 2)
    nq = 2
    kern = functools.partial(_attn_kernel, tq=tq, seq=seq, lambda_init=lambda_init, unroll=ATTN_UNROLL)
    vec = pl.BlockSpec((1, DA_HEAD_DIM), lambda b, h, i: (0, 0))
    qt_shape = (nq, 2, DA_VALUE_DIM, tq)
    return pl.pallas_call(
        kern,
        out_shape=jax.ShapeDtypeStruct((batch, seq, width), BF16),
        grid=(batch, DA_HEADS, seq // (nq * tq)),
        in_specs=[pl.BlockSpec((None, nq * tq, DA_VALUE_DIM), lambda b, h, i: (b, i, h)),
                  pl.BlockSpec((None, seq, DA_VALUE_DIM), lambda b, h, i: (b, 0, DA_HEADS + h)),
                  pl.BlockSpec((None, seq, DA_VALUE_DIM), lambda b, h, i: (b, 0, 2 * DA_HEADS + h)),
                  vec, vec, vec, vec,
                  pl.BlockSpec((1, DA_VALUE_DIM), lambda b, h, i: (0, 0))],
        out_specs=pl.BlockSpec((None, nq * tq, DA_VALUE_DIM), lambda b, h, i: (b, i, h)),
        scratch_shapes=[pltpu.VMEM((seq // tq, DA_VALUE_DIM, tq), BF16),
                        pltpu.VMEM(qt_shape, BF16),
                        pltpu.VMEM((2, 2, tq, tq), F32),
                        pltpu.VMEM((2, 2, 1, tq), F32),
                        pltpu.VMEM((2, DA_VALUE_DIM, tq), F32),
                        pltpu.VMEM((2, 1, tq), F32),
                        pltpu.VMEM((2, 1, tq), F32)],
        compiler_params=_cparams(("parallel", "parallel", "arbitrary"), 48),
        name="diff_attention",
    )(qkv3, qkv3, qkv3, lq1.reshape(1, -1), lk1.reshape(1, -1), lq2.reshape(1, -1),
      lk2.reshape(1, -1), sub_gain.reshape(1, -1)).reshape(batch * seq, width)


def _ssd_kernel(xbc_ref, z_ref, dt_ref, dtb_ref, alog_ref, dsk_ref, ng_ref,
                tril_ref, e128_ref, e64_ref, y_ref, state_sc, *, steps_per_batch, inner):
    L = SSD_CHUNK
    G = SSM_GROUPS
    N = SSM_STATE
    gw = inner // G
    R = gw // SSM_HEAD_DIM

    @pl.when(pl.program_id(0) % steps_per_batch == 0)
    def _():
        state_sc[...] = jnp.zeros_like(state_sc)

    xbc = xbc_ref[...].astype(F32)

    x = dt_ref[...] + dtb_ref[...]
    dt = jnp.maximum(x, 0.0) + jnp.log(1.0 + jnp.exp(-jnp.abs(x)))
    adt = dt * (-jnp.exp(alog_ref[...]))

    acum = _dot3_lhs01(tril_ref[...], adt)
    acol128 = _dot2_rhs01(acum, e128_ref[...])
    acol64 = _dot2_rhs01(acum, e64_ref[...])
    dt64 = _dot(dt.astype(BF16), e64_ref[...])

    ri = lax.broadcasted_iota(jnp.int32, acol128.shape, 0)
    ci = lax.broadcasted_iota(jnp.int32, acol128.shape, 1) & (L - 1)
    arow = jnp.sum(jnp.where(ri == ci, acol128, 0.0), axis=0, keepdims=True)
    decay_in = jnp.exp(jnp.where(ci <= ri, acol128 - arow, -jnp.inf))

    xs = xbc[:, :inner]
    xdt = xs * dt64
    last = acol64[L - 1:L, :]
    xdo = (xdt * jnp.exp(last - acol64)).astype(BF16)
    xdt = xdt.astype(BF16)
    e_a = jnp.exp(acol64)
    e_last = jnp.exp(last)

    rr = _idiv(lax.broadcasted_iota(jnp.int32, (R * L, gw), 0), L)
    cc = _idiv(lax.broadcasted_iota(jnp.int32, (R * L, gw), 1), SSM_HEAD_DIM)
    blockdiag = rr == cc

    for g in range(G):
        bg = xbc[:, inner + g * N: inner + (g + 1) * N]
        cg = xbc[:, inner + G * N + g * N: inner + G * N + (g + 1) * N].astype(BF16)
        bgt = bg.T.astype(BF16)
        cb = _dot(cg, bgt)
        cbt = jnp.concatenate([cb] * R, axis=1)
        mg = (cbt * decay_in[:, g * R * L:(g + 1) * R * L]).astype(BF16)
        xg = xdt[:, g * gw:(g + 1) * gw]
        xbd = jnp.where(blockdiag, jnp.concatenate([xg] * R, axis=0), jnp.zeros((), BF16))
        y = _dot(mg, xbd)
        st = state_sc[g]
        y = y + _dot(cg, st.astype(BF16)) * e_a[:, g * gw:(g + 1) * gw]
        state_sc[g] = st * e_last[:, g * gw:(g + 1) * gw] + _dot(bgt, xdo[:, g * gw:(g + 1) * gw])
        y = y + dsk_ref[:, g * gw:(g + 1) * gw] * xs[:, g * gw:(g + 1) * gw]
        y = y * _silu(z_ref[:, g * gw:(g + 1) * gw].astype(F32))
        ms = jnp.mean(y * y, axis=-1, keepdims=True)
        y_ref[:, g * gw:(g + 1) * gw] = (y * lax.rsqrt(ms + EPS) * ng_ref[:, g * gw:(g + 1) * gw]).astype(y_ref.dtype)


def _mamba2_mixer(x2, g, w_in, conv_w, conv_b, dt_bias, a_log, d_skip, norm_gain, seq):
    t, d = x2.shape
    nh = dt_bias.shape[0]
    inner = nh * SSM_HEAD_DIM
    conv_dim = conv_w.shape[1]
    L = SSD_CHUNK
    w_xz = jnp.concatenate([w_in[:, inner:inner + conv_dim], w_in[:, :inner]], axis=1).astype(BF16)
    w_dt = jnp.pad(w_in[:, inner + conv_dim:], ((0, 0), (0, LANES - nh))).astype(BF16)
    xz, dtr = _inproj_conv(x2, g, w_xz, w_dt, conv_w, conv_b, seq, 1024, 512)

    pad1 = lambda v: jnp.pad(v, (0, LANES - nh)).reshape(1, LANES)
    rows = jnp.arange(L)
    tril = (rows[None, :] <= rows[:, None]).astype(BF16)
    head = jnp.arange(LANES)
    e128 = (head[:, None] == (jnp.arange(nh * L) // L)[None, :]).astype(BF16)
    e64 = (head[:, None] == (jnp.arange(inner) // SSM_HEAD_DIM)[None, :]).astype(BF16)
    dsk = jnp.repeat(d_skip, SSM_HEAD_DIM).reshape(1, inner)

    nblk_x = conv_dim // inner
    kern = functools.partial(_ssd_kernel, steps_per_batch=seq // L, inner=inner)
    const = lambda shape: _resident(shape, lambda i: (0, 0))
    y = pl.pallas_call(
        kern,
        out_shape=jax.ShapeDtypeStruct((t, inner), BF16),
        grid=(t // L,),
        in_specs=[pl.BlockSpec((L, conv_dim), lambda i: (i, 0)),
                  pl.BlockSpec((L, inner), lambda i: (i, nblk_x)),
                  pl.BlockSpec((L, LANES), lambda i: (i, 0)),
                  const((1, LANES)), const((1, LANES)),
                  const((1, inner)), const((1, inner)),
                  const((L, L)), const((LANES, nh * L)), const((LANES, inner))],
        out_specs=pl.BlockSpec((L, inner), lambda i: (i, 0)),
        scratch_shapes=[pltpu.VMEM((SSM_GROUPS, SSM_STATE, inner // SSM_GROUPS), F32)],
        compiler_params=_cparams(("arbitrary",), 48),
        name="ssd_scan",
    )(xz, xz, dtr, pad1(dt_bias), pad1(a_log), dsk,
      norm_gain.reshape(1, -1), tril, e128, e64)
    return y


def _s5_tables(a_re, a_im, log_dt, b_re, b_im, c_re, c_im, d_skip):
    L = S5_CHUNK
    G, P = a_re.shape
    K = b_re.shape[-1]
    lam = lax.complex(jnp.minimum(a_re, -1e-4), a_im)
    delta = jnp.exp(log_dt)[:, None]
    lam_bar = jnp.exp(lam * delta)
    b_bar = ((lam_bar - 1.0) / lam)[..., None] * lax.complex(b_re, b_im)
    c = lax.complex(c_re, c_im)
    tau = jnp.arange(L + 1, dtype=F32)
    pw = jnp.exp(tau[:, None, None] * (lam * delta)[None])

    gs = LANES // K
    ns = G // gs
    eye = jnp.eye(gs, dtype=F32)
    kt = jnp.einsum("gkp,tgp,gpi->gtki", c, pw[:L], b_bar).real
    lag = jnp.arange(L)[None, :] - jnp.arange(L)[:, None]
    blk = jnp.einsum("sgtoi,gh->stgiho", kt.reshape(ns, gs, L, K, K), eye).reshape(ns, L, LANES, LANES)
    m = jnp.where((lag >= 0)[None, :, :, None, None], blk[:, jnp.maximum(lag, 0)], 0.0)
    m = m.transpose(0, 1, 3, 2, 4).reshape(ns, L * LANES, L * LANES)

    bst = pw[:L][::-1].transpose(1, 0, 2)[:, :, None, :] * b_bar.transpose(0, 2, 1)[:, None, :, :]
    bst = bst.reshape(ns, gs, L, K, P)
    put_b = lambda v: jnp.einsum("sgjip,gh->sjgihp", v, eye).reshape(ns, L * LANES, gs * P)
    w = c.transpose(0, 2, 1)[:, :, None, :] * pw[1:].transpose(1, 2, 0)[:, :, :, None]
    w = w.reshape(ns, gs, P, L, K)
    put_c = lambda v: jnp.einsum("sgpto,gh->sgptho", v, eye).reshape(ns, gs * P, L * LANES)
    lam_l = pw[L].reshape(ns, 1, gs * P)
    return dict(
        m=m.astype(BF16),
        bst_re=put_b(bst.real).astype(BF16), bst_im=put_b(bst.imag).astype(BF16),
        cst_re=put_c(w.real).astype(BF16), cst_im=put_c(-w.imag).astype(BF16),
        lam_re=lam_l.real, lam_im=lam_l.imag, dsk=d_skip.reshape(ns, 1, LANES))


def _s5_kernel(u_ref, m_ref, bre_ref, bim_ref, cre_ref, cim_ref, lre_ref, lim_ref, dsk_ref, o_ref,
               a_sc, s_re_sc, s_im_sc, in_re_sc, in_im_sc, carry_sc, *, rows, batch):
    L = S5_CHUNK

    @pl.when(pl.program_id(1) == 0)
    def _():
        carry_sc[...] = jnp.zeros_like(carry_sc)

    for b in range(batch):
        for j in range(L):
            a_sc[b * rows:(b + 1) * rows, j * LANES:(j + 1) * LANES] = (
                u_ref[b, pl.ds(j, rows, stride=L), :].astype(BF16))

    s_re_sc[...] = _dot(a_sc[...], bre_ref[0])
    s_im_sc[...] = _dot(a_sc[...], bim_ref[0])
    lre = lre_ref[0]
    lim = lim_ref[0]

    def body(c, carry):
        out = []
        for b in range(batch):
            sre, sim = carry[b]
            r = b * rows + c
            in_re_sc[pl.ds(r, 1), :] = sre
            in_im_sc[pl.ds(r, 1), :] = sim
            out.append((lre * sre - lim * sim + s_re_sc[pl.ds(r, 1), :],
                        lre * sim + lim * sre + s_im_sc[pl.ds(r, 1), :]))
        return tuple(out)

    init = tuple((carry_sc[2 * b:2 * b + 1, :], carry_sc[2 * b + 1:2 * b + 2, :]) for b in range(batch))
    last = lax.fori_loop(0, rows, body, init)
    for b in range(batch):
        carry_sc[2 * b:2 * b + 1, :] = last[b][0]
        carry_sc[2 * b + 1:2 * b + 2, :] = last[b][1]

    sre = in_re_sc[...].astype(BF16)
    sim = in_im_sc[...].astype(BF16)
    pairw = 2 * LANES
    for p in range(L // 2):
        cols = slice(p * pairw, (p + 1) * pairw)
        kk = (p + 1) * pairw
        y2 = (_dot(a_sc[:, :kk], m_ref[0, :kk, cols])
              + _dot(sre, cre_ref[0, :, cols]) + _dot(sim, cim_ref[0, :, cols]))
        for h in range(2):
            t = 2 * p + h
            for b in range(batch):
                y = (y2[b * rows:(b + 1) * rows, h * LANES:(h + 1) * LANES]
                     + dsk_ref[0] * u_ref[b, pl.ds(t, rows, stride=L), :])
                inner = math.sqrt(2.0 / math.pi) * (y + 0.044715 * (y * y * y))
                o_ref[b, pl.ds(t, rows, stride=L), :] = 0.5 * y * (1.0 + jnp.tanh(inner))


def _s5_mixer(x2, g, a_re, a_im, log_dt, b_re, b_im, c_re, c_im, d_skip, seq):
    t, d = x2.shape
    L = S5_CHUNK
    G, K = d_skip.shape
    tab = _s5_tables(a_re, a_im, log_dt, b_re, b_im, c_re, c_im, d_skip)
    u = _rmsnorm(x2, g, F32)
    ns = d // LANES
    sw = (LANES // K) * S5_STATE
    batch = t // seq
    assert 2 * batch <= SUBLANES
    steps = min(S5_STEPS, seq)
    rows = steps // L
    lw = L * LANES
    kern = functools.partial(_s5_kernel, rows=rows, batch=batch)
    slab = lambda shape: pl.BlockSpec(shape, lambda s, i: (s, 0, 0))
    ya = pl.pallas_call(
        kern,
        out_shape=jax.ShapeDtypeStruct((batch, seq, d), F32),
        grid=(ns, seq // steps),
        in_specs=[pl.BlockSpec((batch, steps, LANES), lambda s, i: (0, i, s)),
                  slab((1, lw, lw)), slab((1, lw, sw)), slab((1, lw, sw)),
                  slab((1, sw, lw)), slab((1, sw, lw)),
                  slab((1, 1, sw)), slab((1, 1, sw)), slab((1, 1, LANES))],
        out_specs=pl.BlockSpec((batch, steps, LANES), lambda s, i: (0, i, s)),
        scratch_shapes=[pltpu.VMEM((batch * rows, lw), BF16)] + [pltpu.VMEM((batch * rows, sw), F32)] * 4
                       + [pltpu.VMEM((SUBLANES, sw), F32)],
        compiler_params=_cparams(("parallel", "arbitrary"), 48),
        name="s5_scan",
    )(u.reshape(batch, seq, d), tab["m"], tab["bst_re"], tab["bst_im"], tab["cst_re"], tab["cst_im"],
      tab["lam_re"], tab["lam_im"], tab["dsk"]).reshape(t, d)
    return ya


def kernel(x, norm_mix, norm_ffn, norm_final, attn_w_qkv, attn_w_o, attn_lambda_q1, attn_lambda_k1, attn_lambda_q2, attn_lambda_k2, attn_sub_gain, ssm_w_in, ssm_conv_w, ssm_conv_b, ssm_dt_bias, ssm_a_log, ssm_d, ssm_norm_gain, ssm_w_out, s5_a_re, s5_a_im, s5_log_dt, s5_b_re, s5_b_im, s5_c_re, s5_c_im, s5_d, s5_w_glu, ffn_w_up, ffn_conv_w, ffn_conv_b, ffn_w_down):
    batch, seq, d = x.shape
    depth = norm_mix.shape[0]
    x2 = x.reshape(batch * seq, d)
    for i in range(depth):
        kind = i % N_MIXERS
        j = i // N_MIXERS
        if kind == 0:
            lambda_init = 0.8 - 0.6 * math.exp(-0.3 * i)
            qkv = _qkv_proj(x2, norm_mix[i], attn_w_qkv[j], seq)
            a = _diff_attention_core(qkv, attn_lambda_q1[j], attn_lambda_k1[j], attn_lambda_q2[j],
                                     attn_lambda_k2[j], attn_sub_gain[j], lambda_init, batch, seq)
            wm = attn_w_o[j]
        elif kind == 1:
            a = _mamba2_mixer(x2, norm_mix[i], ssm_w_in[j], ssm_conv_w[j], ssm_conv_b[j], ssm_dt_bias[j],
                              ssm_a_log[j], ssm_d[j], ssm_norm_gain[j], seq)
            wm = ssm_w_out[j]
        else:
            a = _s5_mixer(x2, norm_mix[i], s5_a_re[j], s5_a_im[j], s5_log_dt[j], s5_b_re[j], s5_b_im[j],
                          s5_c_re[j], s5_c_im[j], s5_d[j], seq)
            wm = s5_w_glu[j]
        x2 = _mixer_out_ffn(x2, a, wm, kind == 2, norm_ffn[i], ffn_w_up[i], ffn_conv_w[i], ffn_conv_b[i],
                            ffn_w_down[i], norm_final, i == depth - 1, seq)
    return x2.reshape(batch, seq, d)
```
